```python
import jax, jax.numpy as jnp
from jax import lax
import numpy as np

D_MODEL = 1024
BATCH = 1
SEQ = 16384
DEPTH = 1
DEC_BATCH = 128
DEC_SEQ = 8
PAST_LEN = 8192
PAGE_SIZE = 128

N_META = 16
ATTN_HEADS = 8
HEAD_DIM = 64
ATTN_WIDTH = ATTN_HEADS * HEAD_DIM
POOL_WINDOWS = (2, 4, 8, 16)
N_POOL_GROUPS = len(POOL_WINDOWS)
POOL_WIDTH = D_MODEL - ATTN_WIDTH
POOL_GROUP_WIDTH = POOL_WIDTH // N_POOL_GROUPS
POOL_HIST = max(POOL_WINDOWS) - 1
MIX_WIDTH = ATTN_WIDTH + POOL_WIDTH
IN_WIDTH = 3 * ATTN_WIDTH + ATTN_HEADS + POOL_WIDTH
D_FF = -(-8 * D_MODEL // (3 * 256)) * 256
Q_BLOCK = 128
EPS = 1e-6
NEG_INF = -1e30

kernel_name = "hymba_fox_poolformer_decode_step"


def rms_norm(x, g):
    xf = x.astype(jnp.float32)
    y = xf * lax.rsqrt(jnp.mean(xf * xf, axis=-1, keepdims=True) + EPS) * g.astype(jnp.float32)
    return y.astype(x.dtype)


def split_projection(h, w_in, b_f):
    z = h @ w_in
    B, T, _ = z.shape
    q, k, v, fl, u = jnp.split(
        z, [ATTN_WIDTH, 2 * ATTN_WIDTH, 3 * ATTN_WIDTH, 3 * ATTN_WIDTH + ATTN_HEADS], axis=-1)
    shp = (B, T, ATTN_HEADS, HEAD_DIM)
    logf = jax.nn.log_sigmoid(fl.astype(jnp.float32) + b_f.astype(jnp.float32))
    return q.reshape(shp), k.reshape(shp), v.reshape(shp), logf, u


def forgetting_attention(q, dq, qpos, k, v, dk, kpos):
    s = jnp.einsum('bqhd,bkhd->bhqk', q, k).astype(jnp.float32) * (HEAD_DIM ** -0.5)
    decay = jnp.swapaxes(dq, 1, 2)[..., :, None] - jnp.swapaxes(dk, 1, 2)[..., None, :]
    s = jnp.where((kpos[None, :] <= qpos[:, None]), s + decay, NEG_INF)
    p = jax.nn.softmax(s, axis=-1)
    return jnp.einsum('bhqk,bkhd->bqhd', p.astype(v.dtype), v)


def pool_mixer(u_ext, pos, w_pool, pool_scale):
    B = u_ext.shape[0]
    T = pos.shape[0]
    uf = u_ext.astype(jnp.float32)
    c = jnp.concatenate([jnp.zeros((B, 1, POOL_WIDTH), jnp.float32), jnp.cumsum(uf, axis=1)], axis=1)
    end = c[:, POOL_HIST + 1:]
    means = []
    for g, w in enumerate(POOL_WINDOWS):
        lo, hi = g * POOL_GROUP_WIDTH, (g + 1) * POOL_GROUP_WIDTH
        start = c[:, POOL_HIST + 1 - w: POOL_HIST + 1 - w + T, lo:hi]
        count = jnp.minimum(w, pos + 1).astype(jnp.float32)
        means.append((end[..., lo:hi] - start) / count[None, :, None])
    diff = (jnp.concatenate(means, axis=-1) - uf[:, POOL_HIST:]).reshape(B, T, N_POOL_GROUPS, POOL_GROUP_WIDTH)
    out = jnp.einsum('btgc,gcd->btgd', diff, w_pool.astype(jnp.float32)).reshape(B, T, POOL_WIDTH)
    return (out * pool_scale.astype(jnp.float32)).astype(u_ext.dtype)


def merge_out(attn_o, pool_o, w_out):
    B, T = pool_o.shape[:2]
    mix = jnp.concatenate([attn_o.reshape(B, T, ATTN_WIDTH), pool_o], axis=-1)
    return mix @ w_out


def swiglu(h, w_gate, w_up, w_down):
    return (jax.nn.silu(h @ w_gate) * (h @ w_up)) @ w_down


def setup_inputs(seed: int = 0) -> dict:
    key = jax.random.key(seed)
    ks = jax.random.split(key, 20)
    n_pages = PAST_LEN // PAGE_SIZE
    n_used = DEC_BATCH * n_pages
    n_phys = (n_used * 5 + 3) // 4
    f32 = jnp.float32
    nrm = lambda k, s: jax.random.normal(k, s, f32)
    page_table = jax.random.permutation(ks[6], n_phys)[:n_used].reshape(DEC_BATCH, n_pages).astype(jnp.int32)
    return {
        "x_prompt": nrm(ks[0], (BATCH, SEQ, D_MODEL)),
        "x_sample": nrm(ks[1], (DEC_BATCH, DEC_SEQ, D_MODEL)),
        "cache_k": nrm(ks[2], (DEPTH, n_phys, PAGE_SIZE, ATTN_HEADS, HEAD_DIM)),
        "cache_v": nrm(ks[3], (DEPTH, n_phys, PAGE_SIZE, ATTN_HEADS, HEAD_DIM)),
        "cache_logf": jax.nn.log_sigmoid(3.0 + nrm(ks[4], (DEPTH, n_phys, PAGE_SIZE, ATTN_HEADS))),
        "state_pool": nrm(ks[5], (DEPTH, DEC_BATCH, POOL_HIST, POOL_WIDTH)),
        "page_table": page_table,
        "meta_tokens": nrm(ks[7], (N_META, D_MODEL)),
        "g_mix": 1.0 + 0.05 * nrm(ks[8], (DEPTH, D_MODEL)),
        "w_in": nrm(ks[9], (DEPTH, D_MODEL, IN_WIDTH)) * D_MODEL ** -0.5,
        "b_f": 3.0 + 0.5 * nrm(ks[10], (DEPTH, ATTN_HEADS)),
        "w_pool": nrm(ks[11], (DEPTH, N_POOL_GROUPS, POOL_GROUP_WIDTH, POOL_GROUP_WIDTH)) * POOL_GROUP_WIDTH ** -0.5,
        "pool_scale": 1.0 + 0.1 * nrm(ks[12], (DEPTH, POOL_WIDTH)),
        "w_out": nrm(ks[13], (DEPTH, MIX_WIDTH, D_MODEL)) * MIX_WIDTH ** -0.5,
        "g_ffn": 1.0 + 0.05 * nrm(ks[14], (DEPTH, D_MODEL)),
        "w_gate": nrm(ks[15], (DEPTH, D_MODEL, D_FF)) * D_MODEL ** -0.5,
        "w_up": nrm(ks[16], (DEPTH, D_MODEL, D_FF)) * D_MODEL ** -0.5,
        "w_down": nrm(ks[17], (DEPTH, D_FF, D_MODEL)) * D_FF ** -0.5,
        "g_final": 1.0 + 0.05 * nrm(ks[18], (D_MODEL,)),
    }


def reference(x_prompt, x_sample, cache_k, cache_v, cache_logf, state_pool, page_table,
              meta_tokens, g_mix, w_in, b_f, w_pool, pool_scale, w_out, g_ffn,
              w_gate, w_up, w_down, g_final):
    B, S = x_prompt.shape[:2]
    L = N_META + S
    dt = x_prompt.dtype
    x = jnp.concatenate([jnp.broadcast_to(meta_tokens.astype(dt)[None], (B, N_META, D_MODEL)), x_prompt], axis=1)
    pos_p = jnp.arange(L)
    n_blk = S // Q_BLOCK
    kp, vp, lfp, pp = [], [], [], []
    for l in range(DEPTH):
        h = rms_norm(x, g_mix[l])
        q, k, v, lf, u = split_projection(h, w_in[l], b_f[l])
        d = jnp.cumsum(lf, axis=1)
        o_meta = forgetting_attention(q[:, :N_META], d[:, :N_META], pos_p[:N_META],
                                      k[:, :N_META], v[:, :N_META], d[:, :N_META], pos_p[:N_META])
        qb = q[:, N_META:].reshape(B, n_blk, Q_BLOCK, ATTN_HEADS, HEAD_DIM).transpose(1, 0, 2, 3, 4)
        db = d[:, N_META:].reshape(B, n_blk, Q_BLOCK, ATTN_HEADS).transpose(1, 0, 2, 3)
        pb = pos_p[N_META:].reshape(n_blk, Q_BLOCK)
        o_blk = lax.map(lambda a, k=k, v=v, d=d: forgetting_attention(a[0], a[1], a[2], k, v, d, pos_p),
                        (qb, db, pb))
        o_real = o_blk.transpose(1, 0, 2, 3, 4).reshape(B, S, ATTN_HEADS, HEAD_DIM)
        attn_o = jnp.concatenate([o_meta, o_real], axis=1)
        u_ext = jnp.concatenate([jnp.zeros((B, POOL_HIST, POOL_WIDTH), u.dtype), u], axis=1)
        pool_o = pool_mixer(u_ext, pos_p, w_pool[l], pool_scale[l])
        x = x + merge_out(attn_o, pool_o, w_out[l])
        x = x + swiglu(rms_norm(x, g_ffn[l]), w_gate[l], w_up[l], w_down[l])
        kp.append(k); vp.append(v); lfp.append(lf); pp.append(u_ext[:, -POOL_HIST:])
    y_prompt = rms_norm(x[:, N_META:], g_final)

    xs = x_sample
    DB, T = xs.shape[:2]
    P = page_table.shape[1] * PAGE_SIZE
    pos_s = P + jnp.arange(T)
    kpos = jnp.arange(P + T)
    ks_, vs_, lfs_, ps_ = [], [], [], []
    for l in range(DEPTH):
        h = rms_norm(xs, g_mix[l])
        q, k, v, lf, u = split_projection(h, w_in[l], b_f[l])
        k_past = cache_k[l][page_table].reshape(DB, P, ATTN_HEADS, HEAD_DIM)
        v_past = cache_v[l][page_table].reshape(DB, P, ATTN_HEADS, HEAD_DIM)
        lf_past = cache_logf[l][page_table].reshape(DB, P, ATTN_HEADS)
        k_all = jnp.concatenate([k_past.astype(k.dtype), k], axis=1)
        v_all = jnp.concatenate([v_past.astype(v.dtype), v], axis=1)
        d_all = jnp.cumsum(jnp.concatenate([lf_past.astype(jnp.float32), lf], axis=1), axis=1)
        attn_o = forgetting_attention(q, d_all[:, P:], pos_s, k_all, v_all, d_all, kpos)
        u_ext = jnp.concatenate([state_pool[l].astype(u.dtype), u], axis=1)
        pool_o = pool_mixer(u_ext, pos_s, w_pool[l], pool_scale[l])
        xs = xs + merge_out(attn_o, pool_o, w_out[l])
        xs = xs + swiglu(rms_norm(xs, g_ffn[l]), w_gate[l], w_up[l], w_down[l])
        ks_.append(k); vs_.append(v); lfs_.append(lf); ps_.append(u_ext[:, -POOL_HIST:])
    y_sample = rms_norm(xs, g_final)

    k_prompt = jnp.stack(kp, 0)
    v_prompt = jnp.stack(vp, 0)
    logf_prompt = jnp.stack(lfp, 0)
    pool_prompt = jnp.stack(pp, 0)
    k_sample = jnp.stack(ks_, 0)
    v_sample = jnp.stack(vs_, 0)
    logf_sample = jnp.stack(lfs_, 0)
    pool_sample = jnp.stack(ps_, 0)
    return (y_prompt, y_sample, k_prompt, v_prompt, logf_prompt, pool_prompt,
            k_sample, v_sample, logf_sample, pool_sample)
```

```python
import functools

import jax
import jax.numpy as jnp
from jax import lax
from jax.experimental import pallas as pl
from jax.experimental.pallas import tpu as pltpu

N_META = 16
ATTN_HEADS = 8
HEAD_DIM = 64
ATTN_WIDTH = ATTN_HEADS * HEAD_DIM
POOL_WINDOWS = (2, 4, 8, 16)
POOL_GROUP_WIDTH = 128
POOL_WIDTH = len(POOL_WINDOWS) * POOL_GROUP_WIDTH
POOL_HIST = max(POOL_WINDOWS) - 1
PAGE_SIZE = 128
EPS = 1e-6
NEG_INF = -1e30

TOKEN_TILE = 256
AUG_DEPTH = 128
V_ROWS = 80
HIST_ROWS = 16
PAGES_PER_STEP = 16
FFN_CHUNK = 256
FFN_ROWS = 512
VMEM_LIMIT_BYTES = 56 * 1024 * 1024

_NT = (((1,), (1,)), ((), ()))


def _rms_norm(x, g):
    return x * lax.rsqrt(jnp.mean(x * x, axis=-1, keepdims=True) + EPS) * g


def _split3(x):
    hi = x.astype(jnp.bfloat16)
    r = x - hi.astype(jnp.float32)
    mid = r.astype(jnp.bfloat16)
    lo = (r - mid.astype(jnp.float32)).astype(jnp.bfloat16)
    return hi, mid, lo


def _exact_dot(ones_mat, x, x_on_left=False):
    out = None
    for part in _split3(x):
        if x_on_left:
            t = jnp.dot(part, ones_mat, preferred_element_type=jnp.float32)
        else:
            t = jnp.dot(ones_mat, part, preferred_element_type=jnp.float32)
        out = t if out is None else out + t
    return out


def _const_spec(shape):
    return pl.BlockSpec(shape, lambda *_: (0,) * len(shape), pipeline_mode=pl.Buffered(1))


def _pool_mix(window_sum_fn, u, pos, w_pool_ref, pool_scale):
    outs = []
    for g, w in enumerate(POOL_WINDOWS):
        lo = g * POOL_GROUP_WIDTH
        cnt = jnp.minimum(w, pos + 1).astype(jnp.float32)
        diff = window_sum_fn(g, w) / cnt - u[:, lo:lo + POOL_GROUP_WIDTH]
        o = jnp.dot(diff.astype(jnp.bfloat16), w_pool_ref[g], preferred_element_type=jnp.float32)
        outs.append(o * pool_scale[:, lo:lo + POOL_GROUP_WIDTH])
    return jnp.concatenate(outs, axis=-1)


def _prompt_proj_kernel(xp_ref, meta_ref, g_ref, wnt_ref, wk_ref, wu_ref, wf_ref, bfc_ref, bfr_ref,
                        sel_ref, wpool_ref, pscale_ref,
                        kt_ref, vt_ref, lft_ref, ka_ref, vta_ref, qta_ref, pool_ref, ulast_ref,
                        xcarry_ref, dcarry_ref, ue_ref, *, n_blocks, seq_len):
    i = pl.program_id(0)
    t = TOKEN_TILE
    g = g_ref[...]

    @pl.when(i == 0)
    def _():
        xcarry_ref[...] = meta_ref[...]
        dcarry_ref[...] = jnp.zeros_like(dcarry_ref)
        hm = _rms_norm(meta_ref[...], g).astype(jnp.bfloat16)
        ue_ref[0:HIST_ROWS, :] = jnp.dot(hm, wu_ref[...], preferred_element_type=jnp.float32)

    xq = xp_ref[...]

    @pl.when(i < n_blocks)
    def _():
        hq = _rms_norm(xq, g).astype(jnp.bfloat16)
        qt = lax.dot_general(wnt_ref[0:ATTN_WIDTH, :], hq, _NT, preferred_element_type=jnp.float32)
        qt = (qt * (HEAD_DIM ** -0.5)).astype(jnp.bfloat16)
        row = lax.broadcasted_iota(jnp.int32, (AUG_DEPTH - HEAD_DIM, t), 0)
        tail = jnp.where(row < 3, 1.0, 0.0).astype(jnp.bfloat16)
        for h in range(ATTN_HEADS):
            qta_ref[h, 0, 0:HEAD_DIM, :] = qt[h * HEAD_DIM:(h + 1) * HEAD_DIM, :]
            qta_ref[h, 0, HEAD_DIM:AUG_DEPTH, :] = tail
        u = jnp.dot(hq, wu_ref[...], preferred_element_type=jnp.float32)
        ue_ref[HIST_ROWS:HIST_ROWS + t, :] = u
        pos = N_META + i * t + lax.broadcasted_iota(jnp.int32, (t, 1), 0)

        def window_sum(grp, w):
            lo = grp * POOL_GROUP_WIDTH
            acc = ue_ref[pl.ds(HIST_ROWS, t), pl.ds(lo, POOL_GROUP_WIDTH)]
            for s in range(1, w):
                acc = acc + ue_ref[pl.ds(HIST_ROWS - s, t), pl.ds(lo, POOL_GROUP_WIDTH)]
            return acc

        pool_ref[...] = _pool_mix(window_sum, u, pos, wpool_ref, pscale_ref[...]).astype(pool_ref.dtype)
        ue_ref[0:HIST_ROWS, :] = u[t - HIST_ROWS:, :]
        ulast_ref[...] = u[t - HIST_ROWS:, :]

    valid = (N_META + seq_len) - i * t
    xk = jnp.concatenate([xcarry_ref[...], xq[0:t - N_META, :]], axis=0)
    rows = lax.broadcasted_iota(jnp.int32, (t, 1), 0)
    xk = jnp.where(rows < valid, xk, 0.0)
    xcarry_ref[...] = xq[t - N_META:, :]
    hk = _rms_norm(xk, g).astype(jnp.bfloat16)
    zt = lax.dot_general(wnt_ref[ATTN_WIDTH:, :], hk, _NT, preferred_element_type=jnp.float32)
    k_t = zt[0:ATTN_WIDTH, :]
    v_t = zt[ATTN_WIDTH:2 * ATTN_WIDTH, :]
    kt_ref[...] = k_t
    vt_ref[...] = v_t
    lft_ref[...] = jax.nn.log_sigmoid(zt[2 * ATTN_WIDTH:2 * ATTN_WIDTH + ATTN_HEADS, :] + bfc_ref[0:ATTN_HEADS, :])
    vrow = lax.broadcasted_iota(jnp.int32, (V_ROWS - HEAD_DIM, t), 0)
    vtail = jnp.where(vrow < 1, 1.0, 0.0).astype(jnp.bfloat16)
    for h in range(ATTN_HEADS):
        vta_ref[h, 0, 0:HEAD_DIM, :] = v_t[h * HEAD_DIM:(h + 1) * HEAD_DIM, :].astype(jnp.bfloat16)
        vta_ref[h, 0, HEAD_DIM:V_ROWS, :] = vtail

    zk = jnp.dot(hk, wk_ref[...], preferred_element_type=jnp.float32)
    lf = jax.nn.log_sigmoid(jnp.dot(hk, wf_ref[...], preferred_element_type=jnp.float32) + bfr_ref[...])
    r_i = lax.broadcasted_iota(jnp.int32, (t, t), 0)
    c_i = lax.broadcasted_iota(jnp.int32, (t, t), 1)
    tri = jnp.where(c_i <= r_i, 1.0, 0.0).astype(jnp.bfloat16)
    d = _exact_dot(tri, lf) + dcarry_ref[...]
    dcarry_ref[...] = d[t - 1:t, :]
    for p, part in enumerate(_split3(-d)):
        zk = zk + jnp.dot(part, sel_ref[p], preferred_element_type=jnp.float32)
    ka = zk.astype(jnp.bfloat16)
    for h in range(ATTN_HEADS):
        ka_ref[h] = ka[:, h * AUG_DEPTH:(h + 1) * AUG_DEPTH]


def _prompt_proj(x_prompt2, meta, g_mix, wnt, wk_aug, wu, wf_pad, bf_col, bf_row, sel, wpool, pscale):
    s, d = x_prompt2.shape
    t = TOKEN_TILE
    nb = s // t
    assert nb * t == s and N_META <= HIST_ROWS <= t
    n_k = nb + 1
    l_tot = N_META + s
    xb = lambda i: (jnp.minimum(i, nb - 1), 0)
    full = lambda *shape: _const_spec(shape)
    kernel = functools.partial(_prompt_proj_kernel, n_blocks=nb, seq_len=s)
    out_shape = (
        jax.ShapeDtypeStruct((ATTN_WIDTH, l_tot), jnp.float32),
        jax.ShapeDtypeStruct((ATTN_WIDTH, l_tot), jnp.float32),
        jax.ShapeDtypeStruct((ATTN_HEADS, l_tot), jnp.float32),
        jax.ShapeDtypeStruct((ATTN_HEADS, n_k * t, AUG_DEPTH), jnp.bfloat16),
        jax.ShapeDtypeStruct((ATTN_HEADS, n_k, V_ROWS, t), jnp.bfloat16),
        jax.ShapeDtypeStruct((ATTN_HEADS, nb, AUG_DEPTH, t), jnp.bfloat16),
        jax.ShapeDtypeStruct((s, POOL_WIDTH), jnp.bfloat16),
        jax.ShapeDtypeStruct((HIST_ROWS, POOL_WIDTH), jnp.float32),
    )
    out_specs = (
        pl.BlockSpec((ATTN_WIDTH, t), lambda i: (0, i)),
        pl.BlockSpec((ATTN_WIDTH, t), lambda i: (0, i)),
        pl.BlockSpec((ATTN_HEADS, t), lambda i: (0, i)),
        pl.BlockSpec((ATTN_HEADS, t, AUG_DEPTH), lambda i: (0, i, 0)),
        pl.BlockSpec((ATTN_HEADS, 1, V_ROWS, t), lambda i: (0, i, 0, 0)),
        pl.BlockSpec((ATTN_HEADS, 1, AUG_DEPTH, t), lambda i: (0, jnp.minimum(i, nb - 1), 0, 0)),
        pl.BlockSpec((t, POOL_WIDTH), xb),
        pl.BlockSpec((HIST_ROWS, POOL_WIDTH), lambda i: (0, 0)),
    )
    in_specs = [
        pl.BlockSpec((t, d), xb),
        full(*meta.shape), full(*g_mix.shape), full(*wnt.shape), full(*wk_aug.shape), full(*wu.shape),
        full(*wf_pad.shape), full(*bf_col.shape), full(*bf_row.shape), full(*sel.shape),
        full(*wpool.shape), full(*pscale.shape),
    ]
    return pl.pallas_call(
        kernel, grid=(n_k,), in_specs=in_specs, out_specs=out_specs, out_shape=out_shape,
        scratch_shapes=[pltpu.VMEM((N_META, d), jnp.float32), pltpu.VMEM((1, 128), jnp.float32),
                        pltpu.VMEM((HIST_ROWS + t, POOL_WIDTH), jnp.float32)],
        compiler_params=pltpu.CompilerParams(dimension_semantics=("arbitrary",),
                                             vmem_limit_bytes=VMEM_LIMIT_BYTES),
        name="prompt_proj",
    )(x_prompt2, meta, g_mix, wnt, wk_aug, wu, wf_pad, bf_col, bf_row, sel, wpool, pscale)


def _prompt_attn_kernel(qta_ref, ka_ref, vta_ref, o_ref):
    qi = pl.program_id(1)
    t = TOKEN_TILE
    qt = qta_ref[0, 0]

    def tile(ki, carry, mask):
        m, acc = carry
        start = pl.multiple_of(ki * t, t)
        s = jnp.dot(ka_ref[0, pl.ds(start, t), :], qt, preferred_element_type=jnp.float32)
        if mask is not None:
            s = jnp.where(mask, s, NEG_INF)
        m_new = jnp.maximum(m, jnp.max(s, axis=0, keepdims=True))
        p = jnp.exp(s - m_new).astype(jnp.bfloat16)
        acc = jnp.exp(m - m_new) * acc + jnp.dot(vta_ref[0, ki], p, preferred_element_type=jnp.float32)
        return m_new, acc

    m0 = jnp.full((1, t), NEG_INF, jnp.float32)
    acc0 = jnp.zeros((V_ROWS, t), jnp.float32)
    carry = lax.fori_loop(0, qi, lambda ki, c: tile(ki, c, None), (m0, acc0))
    r_i = lax.broadcasted_iota(jnp.int32, (t, t), 0)
    c_i = lax.broadcasted_iota(jnp.int32, (t, t), 1)
    carry = tile(qi, carry, r_i <= c_i + N_META)
    m, acc = tile(qi + 1, carry, r_i <= c_i + (N_META - t))
    o_ref[...] = (acc[0:HEAD_DIM, :] / acc[HEAD_DIM:HEAD_DIM + 1, :]).astype(o_ref.dtype)


def _prompt_attn(qta, ka, vta):
    heads, nb, _, t = qta.shape
    n_k = vta.shape[1]
    return pl.pallas_call(
        _prompt_attn_kernel, grid=(heads, nb),
        in_specs=[
            pl.BlockSpec((1, 1, AUG_DEPTH, t), lambda h, q: (h, q, 0, 0)),
            pl.BlockSpec((1, n_k * t, AUG_DEPTH), lambda h, q: (h, 0, 0)),
            pl.BlockSpec((1, n_k, V_ROWS, t), lambda h, q: (h, 0, 0, 0)),
        ],
        out_specs=pl.BlockSpec((HEAD_DIM, t), lambda h, q: (h, q)),
        out_shape=jax.ShapeDtypeStruct((heads * HEAD_DIM, nb * t), jnp.bfloat16),
        compiler_params=pltpu.CompilerParams(dimension_semantics=("arbitrary", "arbitrary"),
                                             vmem_limit_bytes=VMEM_LIMIT_BYTES),
        name="prompt_attn",
    )(qta, ka, vta)


def _sample_proj_kernel(x_ref, hist_ref, g_ref, wqkv_ref, wu_ref, wf_ref, bfr_ref, wpool_ref, pscale_ref,
                        q_ref, k_ref, v_ref, lf_ref, u_ref, pool_ref, ue_ref, *, past_len, dec_seq):
    rows = x_ref.shape[0]
    n_seq = rows // dec_seq
    h = _rms_norm(x_ref[...], g_ref[...]).astype(jnp.bfloat16)
    z = jnp.dot(h, wqkv_ref[...], preferred_element_type=jnp.float32)
    q_ref[...] = z[:, 0:ATTN_WIDTH]
    k_ref[...] = z[:, ATTN_WIDTH:2 * ATTN_WIDTH]
    v_ref[...] = z[:, 2 * ATTN_WIDTH:3 * ATTN_WIDTH]
    fl = jnp.dot(h, wf_ref[...], preferred_element_type=jnp.float32) + bfr_ref[...]
    lf_ref[...] = jax.nn.log_sigmoid(fl)[:, 0:ATTN_HEADS]
    u = jnp.dot(h, wu_ref[...], preferred_element_type=jnp.float32)
    u_ref[...] = u
    ue_ref[:, 0:HIST_ROWS, :] = hist_ref[...]
    ue_ref[:, HIST_ROWS:HIST_ROWS + dec_seq, :] = u.reshape(n_seq, dec_seq, POOL_WIDTH)
    pos = past_len + lax.broadcasted_iota(jnp.int32, (rows, 1), 0) % dec_seq

    def window_sum(grp, w):
        lo = grp * POOL_GROUP_WIDTH
        acc = ue_ref[:, pl.ds(HIST_ROWS, dec_seq), pl.ds(lo, POOL_GROUP_WIDTH)]
        for s in range(1, w):
            acc = acc + ue_ref[:, pl.ds(HIST_ROWS - s, dec_seq), pl.ds(lo, POOL_GROUP_WIDTH)]
        return acc.reshape(rows, POOL_GROUP_WIDTH)

    pool_ref[...] = _pool_mix(window_sum, u, pos, wpool_ref, pscale_ref[...]).astype(pool_ref.dtype)


def _sample_proj(xs2, hist, g_mix, wqkv, wu, wf_pad, bf_row, wpool, pscale, *, past_len, dec_seq):
    rows, d = xs2.shape
    n_seq = rows // dec_seq
    seq_blk = min(n_seq, 64)
    assert n_seq % seq_blk == 0 and dec_seq == 8
    rb = seq_blk * dec_seq
    full = lambda *shape: _const_spec(shape)
    rowblk = lambda w: pl.BlockSpec((rb, w), lambda i: (i, 0))
    f32 = jnp.float32
    out_shape = (
        jax.ShapeDtypeStruct((rows, ATTN_WIDTH), f32), jax.ShapeDtypeStruct((rows, ATTN_WIDTH), f32),
        jax.ShapeDtypeStruct((rows, ATTN_WIDTH), f32), jax.ShapeDtypeStruct((rows, ATTN_HEADS), f32),
        jax.ShapeDtypeStruct((rows, POOL_WIDTH), f32), jax.ShapeDtypeStruct((rows, POOL_WIDTH), jnp.bfloat16),
    )
    kernel = functools.partial(_sample_proj_kernel, past_len=past_len, dec_seq=dec_seq)
    return pl.pallas_call(
        kernel, grid=(n_seq // seq_blk,),
        in_specs=[rowblk(d), pl.BlockSpec((seq_blk, HIST_ROWS, POOL_WIDTH), lambda i: (i, 0, 0)),
                  full(*g_mix.shape), full(*wqkv.shape), full(*wu.shape), full(*wf_pad.shape),
                  full(*bf_row.shape), full(*wpool.shape), full(*pscale.shape)],
        out_specs=(rowblk(ATTN_WIDTH), rowblk(ATTN_WIDTH), rowblk(ATTN_WIDTH), rowblk(ATTN_HEADS),
                   rowblk(POOL_WIDTH), rowblk(POOL_WIDTH)),
        out_shape=out_shape,
        scratch_shapes=[pltpu.VMEM((seq_blk, HIST_ROWS + dec_seq, POOL_WIDTH), f32)],
        compiler_params=pltpu.CompilerParams(dimension_semantics=("arbitrary",),
                                             vmem_limit_bytes=VMEM_LIMIT_BYTES),
        name="sample_proj",
    )(xs2, hist, g_mix, wqkv, wu, wf_pad, bf_row, wpool, pscale)


def _sample_attn_kernel(pt_ref, q_ref, kn_ref, vn_ref, lfn_ref, *rest, n_chunks, dec_seq):
    npg = PAGES_PER_STEP
    k_refs = rest[0:npg]
    v_refs = rest[npg:2 * npg]
    lf_refs = rest[2 * npg:3 * npg]
    o_ref, m_ref, l_ref, acc_ref, dcarry_ref = rest[3 * npg:]
    del pt_ref
    c = pl.program_id(1)
    ht = ATTN_HEADS * dec_seq
    bf16 = jnp.bfloat16

    @pl.when(c == 0)
    def _():
        m_ref[...] = jnp.full_like(m_ref, NEG_INF)
        l_ref[...] = jnp.zeros_like(l_ref)
        acc_ref[...] = jnp.zeros_like(acc_ref)
        dcarry_ref[...] = jnp.zeros_like(dcarry_ref)

    q = q_ref[...] * (HEAD_DIM ** -0.5)
    row_h = lax.broadcasted_iota(jnp.int32, (ht, ATTN_WIDTH), 0) // dec_seq
    col_h = lax.broadcasted_iota(jnp.int32, (ht, ATTN_WIDTH), 1) // HEAD_DIM
    head_mask = row_h == col_h
    qbd = jnp.where(head_mask, jnp.concatenate([q] * ATTN_HEADS, axis=0), 0.0).astype(bf16)

    def expand_heads(d8):
        return jnp.concatenate([jnp.broadcast_to(d8[h:h + 1, :], (dec_seq, d8.shape[1]))
                                for h in range(ATTN_HEADS)], axis=0)

    def online_update(s, pv_fn):
        m_old = m_ref[...]
        m_new = jnp.maximum(m_old, jnp.max(s, axis=1, keepdims=True))
        p = jnp.exp(s - m_new)
        alpha = jnp.exp(m_old - m_new)
        l_ref[...] = alpha * l_ref[...] + jnp.sum(p, axis=1, keepdims=True)
        acc_ref[...] = alpha * acc_ref[...] + pv_fn(p.astype(bf16))
        m_ref[...] = m_new

    lf = jnp.concatenate([r[0] for r in lf_refs], axis=0)
    n = npg * ATTN_HEADS
    a_i = lax.broadcasted_iota(jnp.int32, (PAGE_SIZE, PAGE_SIZE), 0)
    b_i = lax.broadcasted_iota(jnp.int32, (PAGE_SIZE, PAGE_SIZE), 1)
    upper = jnp.where(a_i <= b_i, 1.0, 0.0).astype(bf16)
    cs = _exact_dot(upper, lf, x_on_left=True)
    tot = jnp.broadcast_to(cs[:, PAGE_SIZE - 1:PAGE_SIZE], (n, PAGE_SIZE))
    r_i = lax.broadcasted_iota(jnp.int32, (n, n), 0)
    c_i = lax.broadcasted_iota(jnp.int32, (n, n), 1)
    earlier = jnp.where((c_i % ATTN_HEADS == r_i % ATTN_HEADS) & (c_i < r_i), 1.0, 0.0).astype(bf16)
    offs = _exact_dot(earlier, tot)
    carry = dcarry_ref[...]
    d = cs + offs + jnp.concatenate([carry] * npg, axis=0)
    last = d[n - ATTN_HEADS:n, PAGE_SIZE - 1:PAGE_SIZE]
    dcarry_ref[...] = jnp.broadcast_to(last, (ATTN_HEADS, PAGE_SIZE))

    kt = jnp.concatenate([r[0].astype(bf16) for r in k_refs], axis=1)
    s = jnp.dot(qbd, kt, preferred_element_type=jnp.float32)
    dec = jnp.concatenate([expand_heads(d[p * ATTN_HEADS:(p + 1) * ATTN_HEADS, :]) for p in range(npg)], axis=1)
    s = s - dec
    vt = jnp.concatenate([r[0].astype(bf16) for r in v_refs], axis=1)
    online_update(s, lambda p: lax.dot_general(p, vt, _NT, preferred_element_type=jnp.float32))

    @pl.when(c == n_chunks - 1)
    def _():
        pad = jnp.zeros((PAGE_SIZE - dec_seq, ATTN_WIDTH), bf16)
        kn = jnp.concatenate([kn_ref[...].astype(bf16), pad], axis=0)
        vn = jnp.concatenate([vn_ref[...].astype(bf16), pad], axis=0)
        s_new = lax.dot_general(qbd, kn, _NT, preferred_element_type=jnp.float32)
        dn = _exact_dot(upper, lfn_ref[0], x_on_left=True) + dcarry_ref[...]
        s_new = s_new - expand_heads(dn)
        q_t = lax.broadcasted_iota(jnp.int32, (ht, PAGE_SIZE), 0) % dec_seq
        k_t = lax.broadcasted_iota(jnp.int32, (ht, PAGE_SIZE), 1)
        s_new = jnp.where(k_t <= q_t, s_new, NEG_INF)
        online_update(s_new, lambda p: jnp.dot(p, vn, preferred_element_type=jnp.float32))
        o = jnp.where(head_mask, acc_ref[...] / l_ref[...], 0.0)
        out = o[0:dec_seq, :]
        for h in range(1, ATTN_HEADS):
            out = out + o[h * dec_seq:(h + 1) * dec_seq, :]
        o_ref[...] = out.astype(o_ref.dtype)


def _sample_attn(page_table_flat, q, k_new, v_new, lf_new_t, cache_kt, cache_vt, cache_lft, *, n_pages, dec_seq):
    rows = q.shape[0]
    n_seq = rows // dec_seq
    npg = PAGES_PER_STEP
    assert n_pages % npg == 0
    n_chunks = n_pages // npg
    ht = ATTN_HEADS * dec_seq

    def page_spec(shape, j):
        return pl.BlockSpec((1,) + shape, lambda b, c, pt: (pt[b * n_pages + c * npg + j], 0, 0))

    seqblk = lambda w: pl.BlockSpec((dec_seq, w), lambda b, c, pt: (b, 0))
    in_specs = [seqblk(ATTN_WIDTH), seqblk(ATTN_WIDTH), seqblk(ATTN_WIDTH),
                pl.BlockSpec((1, ATTN_HEADS, PAGE_SIZE), lambda b, c, pt: (b, 0, 0))]
    in_specs += [page_spec((ATTN_WIDTH, PAGE_SIZE), j) for j in range(npg)]
    in_specs += [page_spec((ATTN_WIDTH, PAGE_SIZE), j) for j in range(npg)]
    in_specs += [page_spec((ATTN_HEADS, PAGE_SIZE), j) for j in range(npg)]
    grid_spec = pltpu.PrefetchScalarGridSpec(
        num_scalar_prefetch=1, grid=(n_seq, n_chunks), in_specs=in_specs,
        out_specs=pl.BlockSpec((dec_seq, ATTN_WIDTH), lambda b, c, pt: (b, 0)),
        scratch_shapes=[pltpu.VMEM((ht, 1), jnp.float32), pltpu.VMEM((ht, 1), jnp.float32),
                        pltpu.VMEM((ht, ATTN_WIDTH), jnp.float32),
                        pltpu.VMEM((ATTN_HEADS, PAGE_SIZE), jnp.float32)])
    kernel = functools.partial(_sample_attn_kernel, n_chunks=n_chunks, dec_seq=dec_seq)
    return pl.pallas_call(
        kernel, grid_spec=grid_spec,
        out_shape=jax.ShapeDtypeStruct((rows, ATTN_WIDTH), jnp.bfloat16),
        compiler_params=pltpu.CompilerParams(dimension_semantics=("arbitrary", "arbitrary"),
                                             vmem_limit_bytes=VMEM_LIMIT_BYTES),
        name="sample_attn",
    )(page_table_flat, q, k_new, v_new, lf_new_t, *([cache_kt] * npg), *([cache_vt] * npg), *([cache_lft] * npg))


def _mix_ffn_kernel(x_ref, attn_ref, pool_ref, wo_ref, gf_ref, wg_ref, wu_ref, wd_ref, gl_ref,
                    y_ref, acc_ref, *, attn_transposed):
    bf16 = jnp.bfloat16
    if attn_transposed:
        mo = lax.dot_general(attn_ref[...], wo_ref[0:ATTN_WIDTH, :], (((0,), (0,)), ((), ())),
                             preferred_element_type=jnp.float32)
    else:
        mo = jnp.dot(attn_ref[...], wo_ref[0:ATTN_WIDTH, :], preferred_element_type=jnp.float32)
    mo = mo + jnp.dot(pool_ref[...], wo_ref[ATTN_WIDTH:, :], preferred_element_type=jnp.float32)
    x1 = x_ref[...] + mo
    h = _rms_norm(x1, gf_ref[...]).astype(bf16)
    acc_ref[...] = x1

    def chunk(c, carry):
        gate = jnp.dot(h, wg_ref[c], preferred_element_type=jnp.float32)
        up = jnp.dot(h, wu_ref[c], preferred_element_type=jnp.float32)
        act = (jax.nn.silu(gate) * up).astype(bf16)
        acc_ref[...] += jnp.dot(act, wd_ref[c], preferred_element_type=jnp.float32)
        return carry

    lax.fori_loop(0, wg_ref.shape[0], chunk, 0)
    y_ref[...] = _rms_norm(acc_ref[...], gl_ref[...])


def _mix_ffn(x2, attn, pool, wo, g_ffn, wg3, wu3, wd3, g_final, *, attn_transposed):
    rows, d = x2.shape
    rb = min(FFN_ROWS, rows)
    assert rows % rb == 0
    full = lambda *shape: _const_spec(shape)
    attn_spec = (pl.BlockSpec((ATTN_WIDTH, rb), lambda i: (0, i)) if attn_transposed
                 else pl.BlockSpec((rb, ATTN_WIDTH), lambda i: (i, 0)))
    kernel = functools.partial(_mix_ffn_kernel, attn_transposed=attn_transposed)
    return pl.pallas_call(
        kernel, grid=(rows // rb,),
        in_specs=[pl.BlockSpec((rb, d), lambda i: (i, 0)), attn_spec,
                  pl.BlockSpec((rb, POOL_WIDTH), lambda i: (i, 0)),
                  full(*wo.shape), full(*g_ffn.shape), full(*wg3.shape), full(*wu3.shape), full(*wd3.shape),
                  full(*g_final.shape)],
        out_specs=pl.BlockSpec((rb, d), lambda i: (i, 0)),
        out_shape=jax.ShapeDtypeStruct((rows, d), jnp.float32),
        scratch_shapes=[pltpu.VMEM((rb, d), jnp.float32)],
        compiler_params=pltpu.CompilerParams(dimension_semantics=("arbitrary",),
                                             vmem_limit_bytes=VMEM_LIMIT_BYTES),
        name="mix_ffn_t" if attn_transposed else "mix_ffn",
    )(x2, attn, pool, wo, g_ffn, wg3, wu3, wd3, g_final)


def kernel(x_prompt, x_sample, cache_k, cache_v, cache_logf, state_pool, page_table, meta_tokens, g_mix, w_in,
           b_f, w_pool, pool_scale, w_out, g_ffn, w_gate, w_up, w_down, g_final):
    f32, bf16 = jnp.float32, jnp.bfloat16
    batch, seq, d_model = x_prompt.shape
    dec_batch, dec_seq, _ = x_sample.shape
    depth, n_phys = cache_k.shape[0], cache_k.shape[1]
    n_pages = page_table.shape[1]
    d_ff = w_gate.shape[-1]
    assert batch == 1 and depth == 1 and d_ff % FFN_CHUNK == 0
    a = ATTN_WIDTH
    n_ch = d_ff // FFN_CHUNK

    w = w_in[0]
    w_q, w_k, w_v = w[:, 0:a], w[:, a:2 * a], w[:, 2 * a:3 * a]
    w_f, w_u = w[:, 3 * a:3 * a + ATTN_HEADS], w[:, 3 * a + ATTN_HEADS:]
    wnt = jnp.concatenate([w_q.T, w_k.T, w_v.T, jnp.pad(w_f.T, ((0, 16 - ATTN_HEADS), (0, 0)))], axis=0).astype(bf16)
    wk_aug = jnp.pad(w_k.reshape(d_model, ATTN_HEADS, HEAD_DIM), ((0, 0), (0, 0), (0, AUG_DEPTH - HEAD_DIM)))
    wk_aug = wk_aug.reshape(d_model, ATTN_HEADS * AUG_DEPTH).astype(bf16)
    wu = w_u.astype(bf16)
    wf_pad = jnp.pad(w_f, ((0, 0), (0, 128 - ATTN_HEADS))).astype(bf16)
    wqkv = w[:, 0:3 * a].astype(bf16)
    bf_row = jnp.pad(b_f.astype(f32), ((0, 0), (0, 128 - ATTN_HEADS)))
    bf_col = jnp.pad(b_f.astype(f32).T, ((0, 16 - ATTN_HEADS), (0, 0)))
    lane = jnp.arange(128)[:, None]
    col = jnp.arange(ATTN_HEADS * AUG_DEPTH)[None, :]
    sel = jnp.stack([(col == lane * AUG_DEPTH + HEAD_DIM + p) & (lane < ATTN_HEADS) for p in range(3)])
    sel = sel.astype(bf16)
    wpool = w_pool[0].astype(bf16)
    pscale = pool_scale.astype(f32)
    wo = w_out[0].astype(bf16)
    wg3 = w_gate[0].reshape(d_model, n_ch, FFN_CHUNK).transpose(1, 0, 2).astype(bf16)
    wu3 = w_up[0].reshape(d_model, n_ch, FFN_CHUNK).transpose(1, 0, 2).astype(bf16)
    wd3 = w_down[0].reshape(n_ch, FFN_CHUNK, d_model).astype(bf16)
    g_mix2, g_ffn2, g_final2 = g_mix.astype(f32), g_ffn.astype(f32), g_final.astype(f32)[None, :]

    xp2 = x_prompt[0]
    kt, vt, lft, ka, vta, qta, pool_p, u_last = _prompt_proj(
        xp2, meta_tokens.astype(f32), g_mix2, wnt, wk_aug, wu, wf_pad, bf_col, bf_row, sel, wpool, pscale)
    attn_t = _prompt_attn(qta, ka, vta)
    y_prompt = _mix_ffn(xp2, attn_t, pool_p, wo, g_ffn2, wg3, wu3, wd3, g_final2, attn_transposed=True)[None]
    l_tot = N_META + seq
    k_prompt = kt.reshape(ATTN_HEADS, HEAD_DIM, l_tot).transpose(2, 0, 1)[None, None]
    v_prompt = vt.reshape(ATTN_HEADS, HEAD_DIM, l_tot).transpose(2, 0, 1)[None, None]
    logf_prompt = lft.T[None, None]
    pool_prompt = u_last[HIST_ROWS - POOL_HIST:][None, None]

    rows = dec_batch * dec_seq
    xs2 = x_sample.reshape(rows, d_model)
    hist = jnp.pad(state_pool[0].astype(f32), ((0, 0), (HIST_ROWS - POOL_HIST, 0), (0, 0)))
    q_s, k_s, v_s, lf_s, u_s, pool_s = _sample_proj(
        xs2, hist, g_mix2, wqkv, wu, wf_pad, bf_row, wpool, pscale,
        past_len=n_pages * PAGE_SIZE, dec_seq=dec_seq)
    cache_kt = cache_k[0].transpose(0, 2, 3, 1).reshape(n_phys, a, PAGE_SIZE)
    cache_vt = cache_v[0].transpose(0, 2, 3, 1).reshape(n_phys, a, PAGE_SIZE)
    cache_lft = cache_logf[0].transpose(0, 2, 1)
    lf_new_t = jnp.pad(lf_s.reshape(dec_batch, dec_seq, ATTN_HEADS).transpose(0, 2, 1),
                       ((0, 0), (0, 0), (0, PAGE_SIZE - dec_seq)))
    attn_s = _sample_attn(page_table.reshape(-1).astype(jnp.int32), q_s, k_s, v_s, lf_new_t,
                          cache_kt, cache_vt, cache_lft, n_pages=n_pages, dec_seq=dec_seq)
    y_sample = _mix_ffn(xs2, attn_s, pool_s, wo, g_ffn2, wg3, wu3, wd3, g_final2, attn_transposed=False)
    y_sample = y_sample.reshape(dec_batch, dec_seq, d_model)
    shp = (1, dec_batch, dec_seq, ATTN_HEADS, HEAD_DIM)
    k_sample, v_sample = k_s.reshape(shp), v_s.reshape(shp)
    logf_sample = lf_s.reshape(1, dec_batch, dec_seq, ATTN_HEADS)
    u_ext = jnp.concatenate([state_pool[0].astype(f32), u_s.reshape(dec_batch, dec_seq, POOL_WIDTH)], axis=1)
    pool_sample = u_ext[:, -POOL_HIST:][None]
    return (y_prompt, y_sample, k_prompt, v_prompt, logf_prompt, pool_prompt,
            k_sample, v_sample, logf_sample, pool_sample)
```

```python
import functools

import jax
import jax.numpy as jnp
from jax import lax
from jax.experimental import pallas as pl
from jax.experimental.pallas import tpu as pltpu

N_META = 16
ATTN_HEADS = 8
HEAD_DIM = 64
ATTN_WIDTH = ATTN_HEADS * HEAD_DIM
POOL_WINDOWS = (2, 4, 8, 16)
POOL_GROUP_WIDTH = 128
POOL_WIDTH = len(POOL_WINDOWS) * POOL_GROUP_WIDTH
POOL_HIST = max(POOL_WINDOWS) - 1
PAGE_SIZE = 128
EPS = 1e-6
NEG_INF = -1e30

TOKEN_TILE = 512
HEADS_PER_STEP = 4
AUG_DEPTH = 128
V_ROWS = 80
HIST_ROWS = 16
PAGES_PER_STEP = 32
FFN_CHUNK = 256
FFN_ROWS = 512
VMEM_LIMIT_BYTES = 56 * 1024 * 1024

_NT = (((1,), (1,)), ((), ()))


def _rms_norm(x, g):
    return x * lax.rsqrt(jnp.mean(x * x, axis=-1, keepdims=True) + EPS) * g


def _split3(x):
    hi = x.astype(jnp.bfloat16)
    r = x - hi.astype(jnp.float32)
    mid = r.astype(jnp.bfloat16)
    lo = (r - mid.astype(jnp.float32)).astype(jnp.bfloat16)
    return hi, mid, lo


def _exact_dot(ones_mat, x, x_on_left=False):
    out = None
    for part in _split3(x):
        if x_on_left:
            t = jnp.dot(part, ones_mat, preferred_element_type=jnp.float32)
        else:
            t = jnp.dot(ones_mat, part, preferred_element_type=jnp.float32)
        out = t if out is None else out + t
    return out


def _const_spec(shape):
    return pl.BlockSpec(shape, lambda *_: (0,) * len(shape), pipeline_mode=pl.Buffered(1))


def _pool_mix(window_sum_fn, u, pos, w_pool_ref, pool_scale):
    outs = []
    for g, w in enumerate(POOL_WINDOWS):
        lo = g * POOL_GROUP_WIDTH
        cnt = jnp.minimum(w, pos + 1).astype(jnp.float32)
        diff = window_sum_fn(g, w) / cnt - u[:, lo:lo + POOL_GROUP_WIDTH]
        o = jnp.dot(diff.astype(jnp.bfloat16), w_pool_ref[g], preferred_element_type=jnp.float32)
        outs.append(o * pool_scale[:, lo:lo + POOL_GROUP_WIDTH])
    return jnp.concatenate(outs, axis=-1)


def _prompt_proj_kernel(xp_ref, meta_ref, g_ref, wnt_ref, wk_ref, wu_ref, wf_ref, bfc_ref, bfr_ref,
                        sel_ref, wpool_ref, pscale_ref,
                        kt_ref, vt_ref, lft_ref, ka_ref, vta_ref, qta_ref, pool_ref, ulast_ref,
                        xcarry_ref, dcarry_ref, ue_ref, *, n_blocks, seq_len):
    i = pl.program_id(0)
    t = TOKEN_TILE
    g = g_ref[...]

    @pl.when(i == 0)
    def _():
        xcarry_ref[...] = meta_ref[...]
        dcarry_ref[...] = jnp.zeros_like(dcarry_ref)
        hm = _rms_norm(meta_ref[...], g).astype(jnp.bfloat16)
        ue_ref[0:HIST_ROWS, :] = jnp.dot(hm, wu_ref[...], preferred_element_type=jnp.float32)

    xq = xp_ref[...]

    @pl.when(i < n_blocks)
    def _():
        hq = _rms_norm(xq, g).astype(jnp.bfloat16)
        qt = lax.dot_general(wnt_ref[0:ATTN_WIDTH, :], hq, _NT, preferred_element_type=jnp.float32)
        qt = (qt * (HEAD_DIM ** -0.5)).astype(jnp.bfloat16)
        row = lax.broadcasted_iota(jnp.int32, (AUG_DEPTH - HEAD_DIM, t), 0)
        tail = jnp.where(row < 3, 1.0, 0.0).astype(jnp.bfloat16)
        for h in range(ATTN_HEADS):
            qta_ref[h, 0, 0:HEAD_DIM, :] = qt[h * HEAD_DIM:(h + 1) * HEAD_DIM, :]
            qta_ref[h, 0, HEAD_DIM:AUG_DEPTH, :] = tail
        u = jnp.dot(hq, wu_ref[...], preferred_element_type=jnp.float32)
        ue_ref[HIST_ROWS:HIST_ROWS + t, :] = u
        pos = N_META + i * t + lax.broadcasted_iota(jnp.int32, (t, 1), 0)

        def window_sum(grp, w):
            lo = grp * POOL_GROUP_WIDTH
            acc = ue_ref[pl.ds(HIST_ROWS, t), pl.ds(lo, POOL_GROUP_WIDTH)]
            for s in range(1, w):
                acc = acc + ue_ref[pl.ds(HIST_ROWS - s, t), pl.ds(lo, POOL_GROUP_WIDTH)]
            return acc

        pool_ref[...] = _pool_mix(window_sum, u, pos, wpool_ref, pscale_ref[...]).astype(pool_ref.dtype)
        ue_ref[0:HIST_ROWS, :] = u[t - HIST_ROWS:, :]
        ulast_ref[...] = u[t - HIST_ROWS:, :]

    valid = (N_META + seq_len) - i * t
    xk = jnp.concatenate([xcarry_ref[...], xq[0:t - N_META, :]], axis=0)
    rows = lax.broadcasted_iota(jnp.int32, (t, 1), 0)
    xk = jnp.where(rows < valid, xk, 0.0)
    xcarry_ref[...] = xq[t - N_META:, :]
    hk = _rms_norm(xk, g).astype(jnp.bfloat16)
    zt = lax.dot_general(wnt_ref[ATTN_WIDTH:, :], hk, _NT, preferred_element_type=jnp.float32)
    k_t = zt[0:ATTN_WIDTH, :]
    v_t = zt[ATTN_WIDTH:2 * ATTN_WIDTH, :]
    kt_ref[...] = k_t
    vt_ref[...] = v_t
    lft_ref[...] = jax.nn.log_sigmoid(zt[2 * ATTN_WIDTH:2 * ATTN_WIDTH + ATTN_HEADS, :] + bfc_ref[0:ATTN_HEADS, :])
    vrow = lax.broadcasted_iota(jnp.int32, (V_ROWS - HEAD_DIM, t), 0)
    vtail = jnp.where(vrow < 1, 1.0, 0.0).astype(jnp.bfloat16)
    for h in range(ATTN_HEADS):
        vta_ref[h, 0, 0:HEAD_DIM, :] = v_t[h * HEAD_DIM:(h + 1) * HEAD_DIM, :].astype(jnp.bfloat16)
        vta_ref[h, 0, HEAD_DIM:V_ROWS, :] = vtail

    zk = jnp.dot(hk, wk_ref[...], preferred_element_type=jnp.float32)
    lf = jax.nn.log_sigmoid(jnp.dot(hk, wf_ref[...], preferred_element_type=jnp.float32) + bfr_ref[...])
    r_i = lax.broadcasted_iota(jnp.int32, (t, t), 0)
    c_i = lax.broadcasted_iota(jnp.int32, (t, t), 1)
    tri = jnp.where(c_i <= r_i, 1.0, 0.0).astype(jnp.bfloat16)
    d = _exact_dot(tri, lf) + dcarry_ref[...]
    dcarry_ref[...] = d[t - 1:t, :]
    for p, part in enumerate(_split3(-d)):
        zk = zk + jnp.dot(part, sel_ref[p], preferred_element_type=jnp.float32)
    ka = zk.astype(jnp.bfloat16)
    for h in range(ATTN_HEADS):
        ka_ref[h] = ka[:, h * AUG_DEPTH:(h + 1) * AUG_DEPTH]


def _prompt_proj(x_prompt2, meta, g_mix, wnt, wk_aug, wu, wf_pad, bf_col, bf_row, sel, wpool, pscale):
    s, d = x_prompt2.shape
    t = TOKEN_TILE
    nb = s // t
    assert nb * t == s and N_META <= HIST_ROWS <= t
    n_k = nb + 1
    l_tot = N_META + s
    xb = lambda i: (jnp.minimum(i, nb - 1), 0)
    full = lambda *shape: _const_spec(shape)
    kernel = functools.partial(_prompt_proj_kernel, n_blocks=nb, seq_len=s)
    out_shape = (
        jax.ShapeDtypeStruct((ATTN_WIDTH, l_tot), jnp.float32),
        jax.ShapeDtypeStruct((ATTN_WIDTH, l_tot), jnp.float32),
        jax.ShapeDtypeStruct((ATTN_HEADS, l_tot), jnp.float32),
        jax.ShapeDtypeStruct((ATTN_HEADS, n_k * t, AUG_DEPTH), jnp.bfloat16),
        jax.ShapeDtypeStruct((ATTN_HEADS, n_k, V_ROWS, t), jnp.bfloat16),
        jax.ShapeDtypeStruct((ATTN_HEADS, nb, AUG_DEPTH, t), jnp.bfloat16),
        jax.ShapeDtypeStruct((s, POOL_WIDTH), jnp.bfloat16),
        jax.ShapeDtypeStruct((HIST_ROWS, POOL_WIDTH), jnp.float32),
    )
    out_specs = (
        pl.BlockSpec((ATTN_WIDTH, t), lambda i: (0, i)),
        pl.BlockSpec((ATTN_WIDTH, t), lambda i: (0, i)),
        pl.BlockSpec((ATTN_HEADS, t), lambda i: (0, i)),
        pl.BlockSpec((ATTN_HEADS, t, AUG_DEPTH), lambda i: (0, i, 0)),
        pl.BlockSpec((ATTN_HEADS, 1, V_ROWS, t), lambda i: (0, i, 0, 0)),
        pl.BlockSpec((ATTN_HEADS, 1, AUG_DEPTH, t), lambda i: (0, jnp.minimum(i, nb - 1), 0, 0)),
        pl.BlockSpec((t, POOL_WIDTH), xb),
        pl.BlockSpec((HIST_ROWS, POOL_WIDTH), lambda i: (0, 0)),
    )
    in_specs = [
        pl.BlockSpec((t, d), xb),
        full(*meta.shape), full(*g_mix.shape), full(*wnt.shape), full(*wk_aug.shape), full(*wu.shape),
        full(*wf_pad.shape), full(*bf_col.shape), full(*bf_row.shape), full(*sel.shape),
        full(*wpool.shape), full(*pscale.shape),
    ]
    return pl.pallas_call(
        kernel, grid=(n_k,), in_specs=in_specs, out_specs=out_specs, out_shape=out_shape,
        scratch_shapes=[pltpu.VMEM((N_META, d), jnp.float32), pltpu.VMEM((1, 128), jnp.float32),
                        pltpu.VMEM((HIST_ROWS + t, POOL_WIDTH), jnp.float32)],
        compiler_params=pltpu.CompilerParams(dimension_semantics=("arbitrary",),
                                             vmem_limit_bytes=VMEM_LIMIT_BYTES),
        name="prompt_proj",
    )(x_prompt2, meta, g_mix, wnt, wk_aug, wu, wf_pad, bf_col, bf_row, sel, wpool, pscale)


def _prompt_attn_kernel(qta_ref, ka_ref, vta_ref, o_ref):
    qi = pl.program_id(1)
    t = TOKEN_TILE
    n_h = qta_ref.shape[0]

    def tile(ki, carry, mask):
        start = pl.multiple_of(ki * t, t)
        out = []
        for g in range(n_h):
            m, acc = carry[g]
            s = jnp.dot(ka_ref[g, pl.ds(start, t), :], qta_ref[g, 0],
                        preferred_element_type=jnp.float32)
            if mask is not None:
                s = jnp.where(mask, s, NEG_INF)
            m_new = jnp.maximum(m, jnp.max(s, axis=0, keepdims=True))
            p = jnp.exp(s - m_new).astype(jnp.bfloat16)
            acc = jnp.exp(m - m_new) * acc + jnp.dot(vta_ref[g, ki], p, preferred_element_type=jnp.float32)
            out.append((m_new, acc))
        return tuple(out)

    m0 = jnp.full((1, t), NEG_INF, jnp.float32)
    acc0 = jnp.zeros((V_ROWS, t), jnp.float32)
    carry = lax.fori_loop(0, qi, lambda ki, c: tile(ki, c, None), ((m0, acc0),) * n_h)
    r_i = lax.broadcasted_iota(jnp.int32, (t, t), 0)
    c_i = lax.broadcasted_iota(jnp.int32, (t, t), 1)
    carry = tile(qi, carry, r_i <= c_i + N_META)
    carry = tile(qi + 1, carry, r_i <= c_i + (N_META - t))
    for g in range(n_h):
        acc = carry[g][1]
        o_ref[g * HEAD_DIM:(g + 1) * HEAD_DIM, :] = (
            acc[0:HEAD_DIM, :] / acc[HEAD_DIM:HEAD_DIM + 1, :]).astype(o_ref.dtype)


def _prompt_attn(qta, ka, vta):
    heads, nb, _, t = qta.shape
    n_k = vta.shape[1]
    g = HEADS_PER_STEP
    assert heads % g == 0
    resident = pl.Buffered(1)
    return pl.pallas_call(
        _prompt_attn_kernel, grid=(heads // g, nb),
        in_specs=[
            pl.BlockSpec((g, 1, AUG_DEPTH, t), lambda h, q: (h, q, 0, 0)),
            pl.BlockSpec((g, n_k * t, AUG_DEPTH), lambda h, q: (h, 0, 0), pipeline_mode=resident),
            pl.BlockSpec((g, n_k, V_ROWS, t), lambda h, q: (h, 0, 0, 0), pipeline_mode=resident),
        ],
        out_specs=pl.BlockSpec((g * HEAD_DIM, t), lambda h, q: (h, q)),
        out_shape=jax.ShapeDtypeStruct((heads * HEAD_DIM, nb * t), jnp.bfloat16),
        compiler_params=pltpu.CompilerParams(dimension_semantics=("arbitrary", "arbitrary"),
                                             vmem_limit_bytes=VMEM_LIMIT_BYTES),
        name="prompt_attn",
    )(qta, ka, vta)


def _sample_proj_kernel(x_ref, hist_ref, g_ref, wqkv_ref, wu_ref, wf_ref, bfr_ref, wpool_ref, pscale_ref,
                        q_ref, k_ref, v_ref, lf_ref, u_ref, pool_ref, ue_ref, *, past_len, dec_seq):
    rows = x_ref.shape[0]
    n_seq = rows // dec_seq
    h = _rms_norm(x_ref[...], g_ref[...]).astype(jnp.bfloat16)
    z = jnp.dot(h, wqkv_ref[...], preferred_element_type=jnp.float32)
    q_ref[...] = z[:, 0:ATTN_WIDTH]
    k_ref[...] = z[:, ATTN_WIDTH:2 * ATTN_WIDTH]
    v_ref[...] = z[:, 2 * ATTN_WIDTH:3 * ATTN_WIDTH]
    fl = jnp.dot(h, wf_ref[...], preferred_element_type=jnp.float32) + bfr_ref[...]
    lf_ref[...] = jax.nn.log_sigmoid(fl)[:, 0:ATTN_HEADS]
    u = jnp.dot(h, wu_ref[...], preferred_element_type=jnp.float32)
    u_ref[...] = u
    ue_ref[:, 0:HIST_ROWS, :] = hist_ref[...]
    ue_ref[:, HIST_ROWS:HIST_ROWS + dec_seq, :] = u.reshape(n_seq, dec_seq, POOL_WIDTH)
    pos = past_len + lax.broadcasted_iota(jnp.int32, (rows, 1), 0) % dec_seq

    def window_sum(grp, w):
        lo = grp * POOL_GROUP_WIDTH
        acc = ue_ref[:, pl.ds(HIST_ROWS, dec_seq), pl.ds(lo, POOL_GROUP_WIDTH)]
        for s in range(1, w):
            acc = acc + ue_ref[:, pl.ds(HIST_ROWS - s, dec_seq), pl.ds(lo, POOL_GROUP_WIDTH)]
        return acc.reshape(rows, POOL_GROUP_WIDTH)

    pool_ref[...] = _pool_mix(window_sum, u, pos, wpool_ref, pscale_ref[...]).astype(pool_ref.dtype)


def _sample_proj(xs2, hist, g_mix, wqkv, wu, wf_pad, bf_row, wpool, pscale, *, past_len, dec_seq):
    rows, d = xs2.shape
    n_seq = rows // dec_seq
    seq_blk = min(n_seq, 64)
    assert n_seq % seq_blk == 0 and dec_seq == 8
    rb = seq_blk * dec_seq
    full = lambda *shape: _const_spec(shape)
    rowblk = lambda w: pl.BlockSpec((rb, w), lambda i: (i, 0))
    f32 = jnp.float32
    out_shape = (
        jax.ShapeDtypeStruct((rows, ATTN_WIDTH), f32), jax.ShapeDtypeStruct((rows, ATTN_WIDTH), f32),
        jax.ShapeDtypeStruct((rows, ATTN_WIDTH), f32), jax.ShapeDtypeStruct((rows, ATTN_HEADS), f32),
        jax.ShapeDtypeStruct((rows, POOL_WIDTH), f32), jax.ShapeDtypeStruct((rows, POOL_WIDTH), jnp.bfloat16),
    )
    kernel = functools.partial(_sample_proj_kernel, past_len=past_len, dec_seq=dec_seq)
    return pl.pallas_call(
        kernel, grid=(n_seq // seq_blk,),
        in_specs=[rowblk(d), pl.BlockSpec((seq_blk, HIST_ROWS, POOL_WIDTH), lambda i: (i, 0, 0)),
                  full(*g_mix.shape), full(*wqkv.shape), full(*wu.shape), full(*wf_pad.shape),
                  full(*bf_row.shape), full(*wpool.shape), full(*pscale.shape)],
        out_specs=(rowblk(ATTN_WIDTH), rowblk(ATTN_WIDTH), rowblk(ATTN_WIDTH), rowblk(ATTN_HEADS),
                   rowblk(POOL_WIDTH), rowblk(POOL_WIDTH)),
        out_shape=out_shape,
        scratch_shapes=[pltpu.VMEM((seq_blk, HIST_ROWS + dec_seq, POOL_WIDTH), f32)],
        compiler_params=pltpu.CompilerParams(dimension_semantics=("arbitrary",),
                                             vmem_limit_bytes=VMEM_LIMIT_BYTES),
        name="sample_proj",
    )(xs2, hist, g_mix, wqkv, wu, wf_pad, bf_row, wpool, pscale)


def _sample_attn_kernel(pt_ref, q_ref, kn_ref, vn_ref, lfn_ref, *rest, n_chunks, dec_seq):
    npg = PAGES_PER_STEP
    k_refs = rest[0:npg]
    v_refs = rest[npg:2 * npg]
    lf_refs = rest[2 * npg:3 * npg]
    o_ref, m_ref, l_ref, acc_ref, dcarry_ref = rest[3 * npg:]
    del pt_ref
    c = pl.program_id(1)
    ht = ATTN_HEADS * dec_seq
    bf16 = jnp.bfloat16

    @pl.when(c == 0)
    def _():
        m_ref[...] = jnp.full_like(m_ref, NEG_INF)
        l_ref[...] = jnp.zeros_like(l_ref)
        acc_ref[...] = jnp.zeros_like(acc_ref)
        dcarry_ref[...] = jnp.zeros_like(dcarry_ref)

    q = q_ref[...] * (HEAD_DIM ** -0.5)
    row_h = lax.broadcasted_iota(jnp.int32, (ht, ATTN_WIDTH), 0) // dec_seq
    col_h = lax.broadcasted_iota(jnp.int32, (ht, ATTN_WIDTH), 1) // HEAD_DIM
    head_mask = row_h == col_h
    qbd = jnp.where(head_mask, jnp.concatenate([q] * ATTN_HEADS, axis=0), 0.0).astype(bf16)

    def expand_heads(d8):
        return jnp.concatenate([jnp.broadcast_to(d8[h:h + 1, :], (dec_seq, d8.shape[1]))
                                for h in range(ATTN_HEADS)], axis=0)

    def online_update(s, pv_fn):
        m_old = m_ref[...]
        m_new = jnp.maximum(m_old, jnp.max(s, axis=1, keepdims=True))
        p = jnp.exp(s - m_new)
        alpha = jnp.exp(m_old - m_new)
        l_ref[...] = alpha * l_ref[...] + jnp.sum(p, axis=1, keepdims=True)
        acc_ref[...] = alpha * acc_ref[...] + pv_fn(p.astype(bf16))
        m_ref[...] = m_new

    lf = jnp.concatenate([r[0] for r in lf_refs], axis=0)
    n = npg * ATTN_HEADS
    a_i = lax.broadcasted_iota(jnp.int32, (PAGE_SIZE, PAGE_SIZE), 0)
    b_i = lax.broadcasted_iota(jnp.int32, (PAGE_SIZE, PAGE_SIZE), 1)
    upper = jnp.where(a_i <= b_i, 1.0, 0.0).astype(bf16)
    cs = _exact_dot(upper, lf, x_on_left=True)
    tot = jnp.broadcast_to(cs[:, PAGE_SIZE - 1:PAGE_SIZE], (n, PAGE_SIZE))
    r_i = lax.broadcasted_iota(jnp.int32, (n, n), 0)
    c_i = lax.broadcasted_iota(jnp.int32, (n, n), 1)
    earlier = jnp.where((c_i % ATTN_HEADS == r_i % ATTN_HEADS) & (c_i < r_i), 1.0, 0.0).astype(bf16)
    offs = _exact_dot(earlier, tot)
    carry = dcarry_ref[...]
    d = cs + offs + jnp.concatenate([carry] * npg, axis=0)
    last = d[n - ATTN_HEADS:n, PAGE_SIZE - 1:PAGE_SIZE]
    dcarry_ref[...] = jnp.broadcast_to(last, (ATTN_HEADS, PAGE_SIZE))

    kt = jnp.concatenate([r[0].astype(bf16) for r in k_refs], axis=1)
    s = jnp.dot(qbd, kt, preferred_element_type=jnp.float32)
    dec = jnp.concatenate([expand_heads(d[p * ATTN_HEADS:(p + 1) * ATTN_HEADS, :]) for p in range(npg)], axis=1)
    s = s - dec
    vt = jnp.concatenate([r[0].astype(bf16) for r in v_refs], axis=1)
    online_update(s, lambda p: lax.dot_general(p, vt, _NT, preferred_element_type=jnp.float32))

    @pl.when(c == n_chunks - 1)
    def _():
        pad = jnp.zeros((PAGE_SIZE - dec_seq, ATTN_WIDTH), bf16)
        kn = jnp.concatenate([kn_ref[...].astype(bf16), pad], axis=0)
        vn = jnp.concatenate([vn_ref[...].astype(bf16), pad], axis=0)
        s_new = lax.dot_general(qbd, kn, _NT, preferred_element_type=jnp.float32)
        dn = _exact_dot(upper, lfn_ref[0], x_on_left=True) + dcarry_ref[...]
        s_new = s_new - expand_heads(dn)
        q_t = lax.broadcasted_iota(jnp.int32, (ht, PAGE_SIZE), 0) % dec_seq
        k_t = lax.broadcasted_iota(jnp.int32, (ht, PAGE_SIZE), 1)
        s_new = jnp.where(k_t <= q_t, s_new, NEG_INF)
        online_update(s_new, lambda p: jnp.dot(p, vn, preferred_element_type=jnp.float32))
        o = jnp.where(head_mask, acc_ref[...] / l_ref[...], 0.0)
        out = o[0:dec_seq, :]
        for h in range(1, ATTN_HEADS):
            out = out + o[h * dec_seq:(h + 1) * dec_seq, :]
        o_ref[...] = out.astype(o_ref.dtype)


def _sample_attn(page_table_flat, q, k_new, v_new, lf_new_t, cache_kt, cache_vt, cache_lft, *, n_pages, dec_seq):
    rows = q.shape[0]
    n_seq = rows // dec_seq
    npg = PAGES_PER_STEP
    assert n_pages % npg == 0
    n_chunks = n_pages // npg
    ht = ATTN_HEADS * dec_seq

    def page_spec(shape, j):
        return pl.BlockSpec((1,) + shape, lambda b, c, pt: (pt[b * n_pages + c * npg + j], 0, 0))

    seqblk = lambda w: pl.BlockSpec((dec_seq, w), lambda b, c, pt: (b, 0))
    in_specs = [seqblk(ATTN_WIDTH), seqblk(ATTN_WIDTH), seqblk(ATTN_WIDTH),
                pl.BlockSpec((1, ATTN_HEADS, PAGE_SIZE), lambda b, c, pt: (b, 0, 0))]
    in_specs += [page_spec((ATTN_WIDTH, PAGE_SIZE), j) for j in range(npg)]
    in_specs += [page_spec((ATTN_WIDTH, PAGE_SIZE), j) for j in range(npg)]
    in_specs += [page_spec((ATTN_HEADS, PAGE_SIZE), j) for j in range(npg)]
    grid_spec = pltpu.PrefetchScalarGridSpec(
        num_scalar_prefetch=1, grid=(n_seq, n_chunks), in_specs=in_specs,
        out_specs=pl.BlockSpec((dec_seq, ATTN_WIDTH), lambda b, c, pt: (b, 0)),
        scratch_shapes=[pltpu.VMEM((ht, 1), jnp.float32), pltpu.VMEM((ht, 1), jnp.float32),
                        pltpu.VMEM((ht, ATTN_WIDTH), jnp.float32),
                        pltpu.VMEM((ATTN_HEADS, PAGE_SIZE), jnp.float32)])
    kernel = functools.partial(_sample_attn_kernel, n_chunks=n_chunks, dec_seq=dec_seq)
    return pl.pallas_call(
        kernel, grid_spec=grid_spec,
        out_shape=jax.ShapeDtypeStruct((rows, ATTN_WIDTH), jnp.bfloat16),
        compiler_params=pltpu.CompilerParams(dimension_semantics=("arbitrary", "arbitrary"),
                                             vmem_limit_bytes=VMEM_LIMIT_BYTES),
        name="sample_attn",
    )(page_table_flat, q, k_new, v_new, lf_new_t, *([cache_kt] * npg), *([cache_vt] * npg), *([cache_lft] * npg))


def _mix_ffn_kernel(x_ref, attn_ref, pool_ref, wo_ref, gf_ref, wg_ref, wu_ref, wd_ref, gl_ref,
                    y_ref, acc_ref, *, attn_transposed):
    bf16 = jnp.bfloat16
    if attn_transposed:
        mo = lax.dot_general(attn_ref[...], wo_ref[0:ATTN_WIDTH, :], (((0,), (0,)), ((), ())),
                             preferred_element_type=jnp.float32)
    else:
        mo = jnp.dot(attn_ref[...], wo_ref[0:ATTN_WIDTH, :], preferred_element_type=jnp.float32)
    mo = mo + jnp.dot(pool_ref[...], wo_ref[ATTN_WIDTH:, :], preferred_element_type=jnp.float32)
    x1 = x_ref[...] + mo
    h = _rms_norm(x1, gf_ref[...]).astype(bf16)
    acc_ref[...] = x1

    def chunk(c, carry):
        gate = jnp.dot(h, wg_ref[c], preferred_element_type=jnp.float32)
        up = jnp.dot(h, wu_ref[c], preferred_element_type=jnp.float32)
        act = (jax.nn.silu(gate) * up).astype(bf16)
        acc_ref[...] += jnp.dot(act, wd_ref[c], preferred_element_type=jnp.float32)
        return carry

    lax.fori_loop(0, wg_ref.shape[0], chunk, 0)
    y_ref[...] = _rms_norm(acc_ref[...], gl_ref[...])


def _mix_ffn(x2, attn, pool, wo, g_ffn, wg3, wu3, wd3, g_final, *, attn_transposed):
    rows, d = x2.shape
    rb = min(FFN_ROWS, rows)
    assert rows % rb == 0
    full = lambda *shape: _const_spec(shape)
    attn_spec = (pl.BlockSpec((ATTN_WIDTH, rb), lambda i: (0, i)) if attn_transposed
                 else pl.BlockSpec((rb, ATTN_WIDTH), lambda i: (i, 0)))
    kernel = functools.partial(_mix_ffn_kernel, attn_transposed=attn_transposed)
    return pl.pallas_call(
        kernel, grid=(rows // rb,),
        in_specs=[pl.BlockSpec((rb, d), lambda i: (i, 0)), attn_spec,
                  pl.BlockSpec((rb, POOL_WIDTH), lambda i: (i, 0)),
                  full(*wo.shape), full(*g_ffn.shape), full(*wg3.shape), full(*wu3.shape), full(*wd3.shape),
                  full(*g_final.shape)],
        out_specs=pl.BlockSpec((rb, d), lambda i: (i, 0)),
        out_shape=jax.ShapeDtypeStruct((rows, d), jnp.float32),
        scratch_shapes=[pltpu.VMEM((rb, d), jnp.float32)],
        compiler_params=pltpu.CompilerParams(dimension_semantics=("arbitrary",),
                                             vmem_limit_bytes=VMEM_LIMIT_BYTES),
        name="mix_ffn_t" if attn_transposed else "mix_ffn",
    )(x2, attn, pool, wo, g_ffn, wg3, wu3, wd3, g_final)


def kernel(x_prompt, x_sample, cache_k, cache_v, cache_logf, state_pool, page_table, meta_tokens, g_mix, w_in,
           b_f, w_pool, pool_scale, w_out, g_ffn, w_gate, w_up, w_down, g_final):
    f32, bf16 = jnp.float32, jnp.bfloat16
    batch, seq, d_model = x_prompt.shape
    dec_batch, dec_seq, _ = x_sample.shape
    depth, n_phys = cache_k.shape[0], cache_k.shape[1]
    n_pages = page_table.shape[1]
    d_ff = w_gate.shape[-1]
    assert batch == 1 and depth == 1 and d_ff % FFN_CHUNK == 0
    a = ATTN_WIDTH
    n_ch = d_ff // FFN_CHUNK

    w = w_in[0]
    w_q, w_k, w_v = w[:, 0:a], w[:, a:2 * a], w[:, 2 * a:3 * a]
    w_f, w_u = w[:, 3 * a:3 * a + ATTN_HEADS], w[:, 3 * a + ATTN_HEADS:]
    wnt = jnp.concatenate([w_q.T, w_k.T, w_v.T, jnp.pad(w_f.T, ((0, 16 - ATTN_HEADS), (0, 0)))], axis=0).astype(bf16)
    wk_aug = jnp.pad(w_k.reshape(d_model, ATTN_HEADS, HEAD_DIM), ((0, 0), (0, 0), (0, AUG_DEPTH - HEAD_DIM)))
    wk_aug = wk_aug.reshape(d_model, ATTN_HEADS * AUG_DEPTH).astype(bf16)
    wu = w_u.astype(bf16)
    wf_pad = jnp.pad(w_f, ((0, 0), (0, 128 - ATTN_HEADS))).astype(bf16)
    wqkv = w[:, 0:3 * a].astype(bf16)
    bf_row = jnp.pad(b_f.astype(f32), ((0, 0), (0, 128 - ATTN_HEADS)))
    bf_col = jnp.pad(b_f.astype(f32).T, ((0, 16 - ATTN_HEADS), (0, 0)))
    lane = jnp.arange(128)[:, None]
    col = jnp.arange(ATTN_HEADS * AUG_DEPTH)[None, :]
    sel = jnp.stack([(col == lane * AUG_DEPTH + HEAD_DIM + p) & (lane < ATTN_HEADS) for p in range(3)])
    sel = sel.astype(bf16)
    wpool = w_pool[0].astype(bf16)
    pscale = pool_scale.astype(f32)
    wo = w_out[0].astype(bf16)
    wg3 = w_gate[0].reshape(d_model, n_ch, FFN_CHUNK).transpose(1, 0, 2).astype(bf16)
    wu3 = w_up[0].reshape(d_model, n_ch, FFN_CHUNK).transpose(1, 0, 2).astype(bf16)
    wd3 = w_down[0].reshape(n_ch, FFN_CHUNK, d_model).astype(bf16)
    g_mix2, g_ffn2, g_final2 = g_mix.astype(f32), g_ffn.astype(f32), g_final.astype(f32)[None, :]

    xp2 = x_prompt[0]
    kt, vt, lft, ka, vta, qta, pool_p, u_last = _prompt_proj(
        xp2, meta_tokens.astype(f32), g_mix2, wnt, wk_aug, wu, wf_pad, bf_col, bf_row, sel, wpool, pscale)
    attn_t = _prompt_attn(qta, ka, vta)
    y_prompt = _mix_ffn(xp2, attn_t, pool_p, wo, g_ffn2, wg3, wu3, wd3, g_final2, attn_transposed=True)[None]
    l_tot = N_META + seq
    k_prompt = kt.reshape(ATTN_HEADS, HEAD_DIM, l_tot).transpose(2, 0, 1)[None, None]
    v_prompt = vt.reshape(ATTN_HEADS, HEAD_DIM, l_tot).transpose(2, 0, 1)[None, None]
    logf_prompt = lft.T[None, None]
    pool_prompt = u_last[HIST_ROWS - POOL_HIST:][None, None]

    rows = dec_batch * dec_seq
    xs2 = x_sample.reshape(rows, d_model)
    hist = jnp.pad(state_pool[0].astype(f32), ((0, 0), (HIST_ROWS - POOL_HIST, 0), (0, 0)))
    q_s, k_s, v_s, lf_s, u_s, pool_s = _sample_proj(
        xs2, hist, g_mix2, wqkv, wu, wf_pad, bf_row, wpool, pscale,
        past_len=n_pages * PAGE_SIZE, dec_seq=dec_seq)
    cache_kt = cache_k[0].transpose(0, 2, 3, 1).reshape(n_phys, a, PAGE_SIZE)
    cache_vt = cache_v[0].transpose(0, 2, 3, 1).reshape(n_phys, a, PAGE_SIZE)
    cache_lft = cache_logf[0].transpose(0, 2, 1)
    lf_new_t = jnp.pad(lf_s.reshape(dec_batch, dec_seq, ATTN_HEADS).transpose(0, 2, 1),
                       ((0, 0), (0, 0), (0, PAGE_SIZE - dec_seq)))
    attn_s = _sample_attn(page_table.reshape(-1).astype(jnp.int32), q_s, k_s, v_s, lf_new_t,
                          cache_kt, cache_vt, cache_lft, n_pages=n_pages, dec_seq=dec_seq)
    y_sample = _mix_ffn(xs2, attn_s, pool_s, wo, g_ffn2, wg3, wu3, wd3, g_final2, attn_transposed=False)
    y_sample = y_sample.reshape(dec_batch, dec_seq, d_model)
    shp = (1, dec_batch, dec_seq, ATTN_HEADS, HEAD_DIM)
    k_sample, v_sample = k_s.reshape(shp), v_s.reshape(shp)
    logf_sample = lf_s.reshape(1, dec_batch, dec_seq, ATTN_HEADS)
    u_ext = jnp.concatenate([state_pool[0].astype(f32), u_s.reshape(dec_batch, dec_seq, POOL_WIDTH)], axis=1)
    pool_sample = u_ext[:, -POOL_HIST:][None]
    return (y_prompt, y_sample, k_prompt, v_prompt, logf_prompt, pool_prompt,
            k_sample, v_sample, logf_sample, pool_sample)
```

```python
import functools

import jax
import jax.numpy as jnp
from jax import lax
from jax.experimental import pallas as pl
from jax.experimental.pallas import tpu as pltpu

N_META = 16
ATTN_HEADS = 8
HEAD_DIM = 64
ATTN_WIDTH = ATTN_HEADS * HEAD_DIM
POOL_WINDOWS = (2, 4, 8, 16)
POOL_GROUP_WIDTH = 128
POOL_WIDTH = len(POOL_WINDOWS) * POOL_GROUP_WIDTH
POOL_HIST = max(POOL_WINDOWS) - 1
PAGE_SIZE = 128
EPS = 1e-6
NEG_INF = -1e30
LOG2_E = 1.4426950408889634

TOKEN_TILE = 512
HEADS_PER_STEP = 2
AUG_DEPTH = 128
V_ROWS = 80
HIST_ROWS = 16
PAGES_PER_STEP = 32
FFN_CHUNK = 256
FFN_ROWS = 512
VMEM_LIMIT_BYTES = 56 * 1024 * 1024

_NT = (((1,), (1,)), ((), ()))


def _rms_norm(x, g):
    return x * lax.rsqrt(jnp.mean(x * x, axis=-1, keepdims=True) + EPS) * g


def _split3(x):
    hi = x.astype(jnp.bfloat16)
    r = x - hi.astype(jnp.float32)
    mid = r.astype(jnp.bfloat16)
    lo = (r - mid.astype(jnp.float32)).astype(jnp.bfloat16)
    return hi, mid, lo


def _exact_dot(ones_mat, x, x_on_left=False):
    out = None
    for part in _split3(x):
        if x_on_left:
            t = jnp.dot(part, ones_mat, preferred_element_type=jnp.float32)
        else:
            t = jnp.dot(ones_mat, part, preferred_element_type=jnp.float32)
        out = t if out is None else out + t
    return out


def _const_spec(shape):
    return pl.BlockSpec(shape, lambda *_: (0,) * len(shape), pipeline_mode=pl.Buffered(1))


def _pool_mix(window_sum_fn, u, pos, w_pool_ref, pool_scale):
    outs = []
    for g, w in enumerate(POOL_WINDOWS):
        lo = g * POOL_GROUP_WIDTH
        cnt = jnp.minimum(w, pos + 1).astype(jnp.float32)
        diff = window_sum_fn(g, w) / cnt - u[:, lo:lo + POOL_GROUP_WIDTH]
        o = jnp.dot(diff.astype(jnp.bfloat16), w_pool_ref[g], preferred_element_type=jnp.float32)
        outs.append(o * pool_scale[:, lo:lo + POOL_GROUP_WIDTH])
    return jnp.concatenate(outs, axis=-1)


def _prompt_proj_kernel(xp_ref, meta_ref, g_ref, wnt_ref, wk_ref, wu_ref, wf_ref, bfc_ref, bfr_ref,
                        sel_ref, wpool_ref, pscale_ref,
                        kt_ref, vt_ref, lft_ref, ka_ref, vta_ref, qta_ref, pool_ref, ulast_ref,
                        xcarry_ref, dcarry_ref, ue_ref, *, n_blocks, seq_len):
    i = pl.program_id(0)
    t = TOKEN_TILE
    g = g_ref[...]

    @pl.when(i == 0)
    def _():
        xcarry_ref[...] = meta_ref[...]
        dcarry_ref[...] = jnp.zeros_like(dcarry_ref)
        hm = _rms_norm(meta_ref[...], g).astype(jnp.bfloat16)
        ue_ref[0:HIST_ROWS, :] = jnp.dot(hm, wu_ref[...], preferred_element_type=jnp.float32)

    xq = xp_ref[...]

    @pl.when(i < n_blocks)
    def _():
        hq = _rms_norm(xq, g).astype(jnp.bfloat16)
        qt = lax.dot_general(wnt_ref[0:ATTN_WIDTH, :], hq, _NT, preferred_element_type=jnp.float32)
        qt = (qt * (HEAD_DIM ** -0.5 * LOG2_E)).astype(jnp.bfloat16)
        row = lax.broadcasted_iota(jnp.int32, (AUG_DEPTH - HEAD_DIM, t), 0)
        tail = jnp.where(row < 3, 1.0, 0.0).astype(jnp.bfloat16)
        for h in range(ATTN_HEADS):
            qta_ref[h, 0, 0:HEAD_DIM, :] = qt[h * HEAD_DIM:(h + 1) * HEAD_DIM, :]
            qta_ref[h, 0, HEAD_DIM:AUG_DEPTH, :] = tail
        u = jnp.dot(hq, wu_ref[...], preferred_element_type=jnp.float32)
        ue_ref[HIST_ROWS:HIST_ROWS + t, :] = u
        pos = N_META + i * t + lax.broadcasted_iota(jnp.int32, (t, 1), 0)

        def window_sum(grp, w):
            lo = grp * POOL_GROUP_WIDTH
            acc = ue_ref[pl.ds(HIST_ROWS, t), pl.ds(lo, POOL_GROUP_WIDTH)]
            for s in range(1, w):
                acc = acc + ue_ref[pl.ds(HIST_ROWS - s, t), pl.ds(lo, POOL_GROUP_WIDTH)]
            return acc

        pool_ref[...] = _pool_mix(window_sum, u, pos, wpool_ref, pscale_ref[...]).astype(pool_ref.dtype)
        ue_ref[0:HIST_ROWS, :] = u[t - HIST_ROWS:, :]
        ulast_ref[...] = u[t - HIST_ROWS:, :]

    valid = (N_META + seq_len) - i * t
    xk = jnp.concatenate([xcarry_ref[...], xq[0:t - N_META, :]], axis=0)
    rows = lax.broadcasted_iota(jnp.int32, (t, 1), 0)
    xk = jnp.where(rows < valid, xk, 0.0)
    xcarry_ref[...] = xq[t - N_META:, :]
    hk = _rms_norm(xk, g).astype(jnp.bfloat16)
    zt = lax.dot_general(wnt_ref[ATTN_WIDTH:, :], hk, _NT, preferred_element_type=jnp.float32)
    k_t = zt[0:ATTN_WIDTH, :]
    v_t = zt[ATTN_WIDTH:2 * ATTN_WIDTH, :]
    kt_ref[...] = k_t
    vt_ref[...] = v_t
    lft_ref[...] = jax.nn.log_sigmoid(zt[2 * ATTN_WIDTH:2 * ATTN_WIDTH + ATTN_HEADS, :] + bfc_ref[0:ATTN_HEADS, :])
    vrow = lax.broadcasted_iota(jnp.int32, (V_ROWS - HEAD_DIM, t), 0)
    vtail = jnp.where(vrow < 1, 1.0, 0.0).astype(jnp.bfloat16)
    for h in range(ATTN_HEADS):
        vta_ref[h, 0, 0:HEAD_DIM, :] = v_t[h * HEAD_DIM:(h + 1) * HEAD_DIM, :].astype(jnp.bfloat16)
        vta_ref[h, 0, HEAD_DIM:V_ROWS, :] = vtail

    zk = jnp.dot(hk, wk_ref[...], preferred_element_type=jnp.float32)
    lf = jax.nn.log_sigmoid(jnp.dot(hk, wf_ref[...], preferred_element_type=jnp.float32) + bfr_ref[...])
    r_i = lax.broadcasted_iota(jnp.int32, (t, t), 0)
    c_i = lax.broadcasted_iota(jnp.int32, (t, t), 1)
    tri = jnp.where(c_i <= r_i, 1.0, 0.0).astype(jnp.bfloat16)
    d = _exact_dot(tri, lf) + dcarry_ref[...]
    dcarry_ref[...] = d[t - 1:t, :]
    for p, part in enumerate(_split3(-d * LOG2_E)):
        zk = zk + jnp.dot(part, sel_ref[p], preferred_element_type=jnp.float32)
    ka = zk.astype(jnp.bfloat16)
    for h in range(ATTN_HEADS):
        ka_ref[h] = ka[:, h * AUG_DEPTH:(h + 1) * AUG_DEPTH]


def _prompt_proj(x_prompt2, meta, g_mix, wnt, wk_aug, wu, wf_pad, bf_col, bf_row, sel, wpool, pscale):
    s, d = x_prompt2.shape
    t = TOKEN_TILE
    nb = s // t
    assert nb * t == s and N_META <= HIST_ROWS <= t
    n_k = nb + 1
    l_tot = N_META + s
    xb = lambda i: (jnp.minimum(i, nb - 1), 0)
    full = lambda *shape: _const_spec(shape)
    kernel = functools.partial(_prompt_proj_kernel, n_blocks=nb, seq_len=s)
    out_shape = (
        jax.ShapeDtypeStruct((ATTN_WIDTH, l_tot), jnp.float32),
        jax.ShapeDtypeStruct((ATTN_WIDTH, l_tot), jnp.float32),
        jax.ShapeDtypeStruct((ATTN_HEADS, l_tot), jnp.float32),
        jax.ShapeDtypeStruct((ATTN_HEADS, n_k * t, AUG_DEPTH), jnp.bfloat16),
        jax.ShapeDtypeStruct((ATTN_HEADS, n_k, V_ROWS, t), jnp.bfloat16),
        jax.ShapeDtypeStruct((ATTN_HEADS, nb, AUG_DEPTH, t), jnp.bfloat16),
        jax.ShapeDtypeStruct((s, POOL_WIDTH), jnp.bfloat16),
        jax.ShapeDtypeStruct((HIST_ROWS, POOL_WIDTH), jnp.float32),
    )
    out_specs = (
        pl.BlockSpec((ATTN_WIDTH, t), lambda i: (0, i)),
        pl.BlockSpec((ATTN_WIDTH, t), lambda i: (0, i)),
        pl.BlockSpec((ATTN_HEADS, t), lambda i: (0, i)),
        pl.BlockSpec((ATTN_HEADS, t, AUG_DEPTH), lambda i: (0, i, 0)),
        pl.BlockSpec((ATTN_HEADS, 1, V_ROWS, t), lambda i: (0, i, 0, 0)),
        pl.BlockSpec((ATTN_HEADS, 1, AUG_DEPTH, t), lambda i: (0, jnp.minimum(i, nb - 1), 0, 0)),
        pl.BlockSpec((t, POOL_WIDTH), xb),
        pl.BlockSpec((HIST_ROWS, POOL_WIDTH), lambda i: (0, 0)),
    )
    in_specs = [
        pl.BlockSpec((t, d), xb),
        full(*meta.shape), full(*g_mix.shape), full(*wnt.shape), full(*wk_aug.shape), full(*wu.shape),
        full(*wf_pad.shape), full(*bf_col.shape), full(*bf_row.shape), full(*sel.shape),
        full(*wpool.shape), full(*pscale.shape),
    ]
    return pl.pallas_call(
        kernel, grid=(n_k,), in_specs=in_specs, out_specs=out_specs, out_shape=out_shape,
        scratch_shapes=[pltpu.VMEM((N_META, d), jnp.float32), pltpu.VMEM((1, 128), jnp.float32),
                        pltpu.VMEM((HIST_ROWS + t, POOL_WIDTH), jnp.float32)],
        compiler_params=pltpu.CompilerParams(dimension_semantics=("arbitrary",),
                                             vmem_limit_bytes=VMEM_LIMIT_BYTES),
        name="prompt_proj",
    )(x_prompt2, meta, g_mix, wnt, wk_aug, wu, wf_pad, bf_col, bf_row, sel, wpool, pscale)


def _prompt_attn_kernel(qta_ref, ka_ref, vta_ref, o_ref,
                        s_a, s_b, p_a, p_b, al_a, al_b, m_ref, acc_ref):
    qi = pl.program_id(1)
    t = TOKEN_TILE
    n_h = qta_ref.shape[0]
    bufs = ((s_a, p_a, al_a), (s_b, p_b, al_b))

    def logits(k, s_out):
        start = pl.multiple_of(k * t, t)
        for g in range(n_h):
            s_out[g] = jnp.dot(ka_ref[g, pl.ds(start, t), :], qta_ref[g, 0],
                               preferred_element_type=jnp.float32)

    def softmax(s_in, p_out, al_out, mask):
        for g in range(n_h):
            s = s_in[g]
            if mask is not None:
                s = jnp.where(mask, s, NEG_INF)
            m_old = m_ref[g]
            m_new = jnp.maximum(m_old, jnp.max(s, axis=0, keepdims=True))
            p_out[g] = jnp.exp2(s - m_new).astype(jnp.bfloat16)
            al_out[g] = jnp.exp2(m_old - m_new)
            m_ref[g] = m_new

    def values(k, p_in, al_in):
        for g in range(n_h):
            acc_ref[g] = al_in[g] * acc_ref[g] + jnp.dot(vta_ref[g, k], p_in[g],
                                                         preferred_element_type=jnp.float32)

    def step(j, par, mask, with_logits=True):
        s_cur, p_cur, al_cur = bufs[par]
        s_nxt, p_nxt, al_nxt = bufs[1 - par]
        if with_logits:
            logits(j + 1, s_nxt)
        softmax(s_cur, p_cur, al_cur, mask)
        values(jnp.maximum(j - 1, 0), p_nxt, al_nxt)

    m_ref[...] = jnp.full_like(m_ref, NEG_INF)
    acc_ref[...] = jnp.zeros_like(acc_ref)
    p_b[...] = jnp.zeros_like(p_b)
    al_b[...] = jnp.ones_like(al_b)
    logits(0, s_a)

    def pair(a, carry):
        step(2 * a, 0, None)
        step(2 * a + 1, 1, None)
        return carry

    n_pairs = qi // 2
    lax.fori_loop(0, n_pairs, pair, 0)
    first = 2 * n_pairs
    diff = lax.broadcasted_iota(jnp.int32, (t, t), 0) - lax.broadcasted_iota(jnp.int32, (t, t), 1)

    def visible(k):
        return diff <= N_META + (qi - k) * t

    step(first, 0, visible(first))
    step(first + 1, 1, visible(first + 1))
    step(first + 2, 0, visible(first + 2), with_logits=False)
    values(first + 2, p_a, al_a)

    for g in range(n_h):
        acc = acc_ref[g]
        o_ref[g * HEAD_DIM:(g + 1) * HEAD_DIM, :] = (
            acc[0:HEAD_DIM, :] / acc[HEAD_DIM:HEAD_DIM + 1, :]).astype(o_ref.dtype)


def _prompt_attn(qta, ka, vta):
    heads, nb, _, t = qta.shape
    n_k = vta.shape[1]
    g = HEADS_PER_STEP
    assert heads % g == 0
    resident = pl.Buffered(1)
    return pl.pallas_call(
        _prompt_attn_kernel, grid=(heads // g, nb),
        in_specs=[
            pl.BlockSpec((g, 1, AUG_DEPTH, t), lambda h, q: (h, q, 0, 0)),
            pl.BlockSpec((g, n_k * t, AUG_DEPTH), lambda h, q: (h, 0, 0), pipeline_mode=resident),
            pl.BlockSpec((g, n_k, V_ROWS, t), lambda h, q: (h, 0, 0, 0), pipeline_mode=resident),
        ],
        out_specs=pl.BlockSpec((g * HEAD_DIM, t), lambda h, q: (h, q)),
        out_shape=jax.ShapeDtypeStruct((heads * HEAD_DIM, nb * t), jnp.bfloat16),
        scratch_shapes=[pltpu.VMEM((g, t, t), jnp.float32), pltpu.VMEM((g, t, t), jnp.float32),
                        pltpu.VMEM((g, t, t), jnp.bfloat16), pltpu.VMEM((g, t, t), jnp.bfloat16),
                        pltpu.VMEM((g, 1, t), jnp.float32), pltpu.VMEM((g, 1, t), jnp.float32),
                        pltpu.VMEM((g, 1, t), jnp.float32), pltpu.VMEM((g, V_ROWS, t), jnp.float32)],
        compiler_params=pltpu.CompilerParams(dimension_semantics=("arbitrary", "arbitrary"),
                                             vmem_limit_bytes=VMEM_LIMIT_BYTES),
        name="prompt_attn",
    )(qta, ka, vta)


def _sample_proj_kernel(x_ref, hist_ref, g_ref, wqkv_ref, wu_ref, wf_ref, bfr_ref, wpool_ref, pscale_ref,
                        q_ref, k_ref, v_ref, lf_ref, u_ref, pool_ref, ue_ref, *, past_len, dec_seq):
    rows = x_ref.shape[0]
    n_seq = rows // dec_seq
    h = _rms_norm(x_ref[...], g_ref[...]).astype(jnp.bfloat16)
    z = jnp.dot(h, wqkv_ref[...], preferred_element_type=jnp.float32)
    q_ref[...] = z[:, 0:ATTN_WIDTH]
    k_ref[...] = z[:, ATTN_WIDTH:2 * ATTN_WIDTH]
    v_ref[...] = z[:, 2 * ATTN_WIDTH:3 * ATTN_WIDTH]
    fl = jnp.dot(h, wf_ref[...], preferred_element_type=jnp.float32) + bfr_ref[...]
    lf_ref[...] = jax.nn.log_sigmoid(fl)[:, 0:ATTN_HEADS]
    u = jnp.dot(h, wu_ref[...], preferred_element_type=jnp.float32)
    u_ref[...] = u
    ue_ref[:, 0:HIST_ROWS, :] = hist_ref[...]
    ue_ref[:, HIST_ROWS:HIST_ROWS + dec_seq, :] = u.reshape(n_seq, dec_seq, POOL_WIDTH)
    pos = past_len + lax.broadcasted_iota(jnp.int32, (rows, 1), 0) % dec_seq

    def window_sum(grp, w):
        lo = grp * POOL_GROUP_WIDTH
        acc = ue_ref[:, pl.ds(HIST_ROWS, dec_seq), pl.ds(lo, POOL_GROUP_WIDTH)]
        for s in range(1, w):
            acc = acc + ue_ref[:, pl.ds(HIST_ROWS - s, dec_seq), pl.ds(lo, POOL_GROUP_WIDTH)]
        return acc.reshape(rows, POOL_GROUP_WIDTH)

    pool_ref[...] = _pool_mix(window_sum, u, pos, wpool_ref, pscale_ref[...]).astype(pool_ref.dtype)


def _sample_proj(xs2, hist, g_mix, wqkv, wu, wf_pad, bf_row, wpool, pscale, *, past_len, dec_seq):
    rows, d = xs2.shape
    n_seq = rows // dec_seq
    seq_blk = min(n_seq, 64)
    assert n_seq % seq_blk == 0 and dec_seq == 8
    rb = seq_blk * dec_seq
    full = lambda *shape: _const_spec(shape)
    rowblk = lambda w: pl.BlockSpec((rb, w), lambda i: (i, 0))
    f32 = jnp.float32
    out_shape = (
        jax.ShapeDtypeStruct((rows, ATTN_WIDTH), f32), jax.ShapeDtypeStruct((rows, ATTN_WIDTH), f32),
        jax.ShapeDtypeStruct((rows, ATTN_WIDTH), f32), jax.ShapeDtypeStruct((rows, ATTN_HEADS), f32),
        jax.ShapeDtypeStruct((rows, POOL_WIDTH), f32), jax.ShapeDtypeStruct((rows, POOL_WIDTH), jnp.bfloat16),
    )
    kernel = functools.partial(_sample_proj_kernel, past_len=past_len, dec_seq=dec_seq)
    return pl.pallas_call(
        kernel, grid=(n_seq // seq_blk,),
        in_specs=[rowblk(d), pl.BlockSpec((seq_blk, HIST_ROWS, POOL_WIDTH), lambda i: (i, 0, 0)),
                  full(*g_mix.shape), full(*wqkv.shape), full(*wu.shape), full(*wf_pad.shape),
                  full(*bf_row.shape), full(*wpool.shape), full(*pscale.shape)],
        out_specs=(rowblk(ATTN_WIDTH), rowblk(ATTN_WIDTH), rowblk(ATTN_WIDTH), rowblk(ATTN_HEADS),
                   rowblk(POOL_WIDTH), rowblk(POOL_WIDTH)),
        out_shape=out_shape,
        scratch_shapes=[pltpu.VMEM((seq_blk, HIST_ROWS + dec_seq, POOL_WIDTH), f32)],
        compiler_params=pltpu.CompilerParams(dimension_semantics=("arbitrary",),
                                             vmem_limit_bytes=VMEM_LIMIT_BYTES),
        name="sample_proj",
    )(xs2, hist, g_mix, wqkv, wu, wf_pad, bf_row, wpool, pscale)


def _sample_attn_kernel(pt_ref, q_ref, kn_ref, vn_ref, lfn_ref, *rest, n_chunks, dec_seq):
    npg = PAGES_PER_STEP
    k_refs = rest[0:npg]
    v_refs = rest[npg:2 * npg]
    lf_refs = rest[2 * npg:3 * npg]
    o_ref, m_ref, l_ref, acc_ref, dcarry_ref = rest[3 * npg:]
    del pt_ref
    c = pl.program_id(1)
    ht = ATTN_HEADS * dec_seq
    bf16 = jnp.bfloat16

    @pl.when(c == 0)
    def _():
        m_ref[...] = jnp.full_like(m_ref, NEG_INF)
        l_ref[...] = jnp.zeros_like(l_ref)
        acc_ref[...] = jnp.zeros_like(acc_ref)
        dcarry_ref[...] = jnp.zeros_like(dcarry_ref)

    q = q_ref[...] * (HEAD_DIM ** -0.5)
    row_h = lax.broadcasted_iota(jnp.int32, (ht, ATTN_WIDTH), 0) // dec_seq
    col_h = lax.broadcasted_iota(jnp.int32, (ht, ATTN_WIDTH), 1) // HEAD_DIM
    head_mask = row_h == col_h
    qbd = jnp.where(head_mask, jnp.concatenate([q] * ATTN_HEADS, axis=0), 0.0).astype(bf16)

    def expand_heads(d8):
        return jnp.concatenate([jnp.broadcast_to(d8[h:h + 1, :], (dec_seq, d8.shape[1]))
                                for h in range(ATTN_HEADS)], axis=0)

    def online_update(s, pv_fn):
        m_old = m_ref[...]
        m_new = jnp.maximum(m_old, jnp.max(s, axis=1, keepdims=True))
        p = jnp.exp(s - m_new)
        alpha = jnp.exp(m_old - m_new)
        l_ref[...] = alpha * l_ref[...] + jnp.sum(p, axis=1, keepdims=True)
        acc_ref[...] = alpha * acc_ref[...] + pv_fn(p.astype(bf16))
        m_ref[...] = m_new

    lf = jnp.concatenate([r[0] for r in lf_refs], axis=0)
    n = npg * ATTN_HEADS
    a_i = lax.broadcasted_iota(jnp.int32, (PAGE_SIZE, PAGE_SIZE), 0)
    b_i = lax.broadcasted_iota(jnp.int32, (PAGE_SIZE, PAGE_SIZE), 1)
    upper = jnp.where(a_i <= b_i, 1.0, 0.0).astype(bf16)
    cs = _exact_dot(upper, lf, x_on_left=True)
    tot = jnp.broadcast_to(cs[:, PAGE_SIZE - 1:PAGE_SIZE], (n, PAGE_SIZE))
    r_i = lax.broadcasted_iota(jnp.int32, (n, n), 0)
    c_i = lax.broadcasted_iota(jnp.int32, (n, n), 1)
    earlier = jnp.where((c_i % ATTN_HEADS == r_i % ATTN_HEADS) & (c_i < r_i), 1.0, 0.0).astype(bf16)
    offs = _exact_dot(earlier, tot)
    carry = dcarry_ref[...]
    d = cs + offs + jnp.concatenate([carry] * npg, axis=0)
    last = d[n - ATTN_HEADS:n, PAGE_SIZE - 1:PAGE_SIZE]
    dcarry_ref[...] = jnp.broadcast_to(last, (ATTN_HEADS, PAGE_SIZE))

    kt = jnp.concatenate([r[0].astype(bf16) for r in k_refs], axis=1)
    s = jnp.dot(qbd, kt, preferred_element_type=jnp.float32)
    dec = jnp.concatenate([expand_heads(d[p * ATTN_HEADS:(p + 1) * ATTN_HEADS, :]) for p in range(npg)], axis=1)
    s = s - dec
    vt = jnp.concatenate([r[0].astype(bf16) for r in v_refs], axis=1)
    online_update(s, lambda p: lax.dot_general(p, vt, _NT, preferred_element_type=jnp.float32))

    @pl.when(c == n_chunks - 1)
    def _():
        pad = jnp.zeros((PAGE_SIZE - dec_seq, ATTN_WIDTH), bf16)
        kn = jnp.concatenate([kn_ref[...].astype(bf16), pad], axis=0)
        vn = jnp.concatenate([vn_ref[...].astype(bf16), pad], axis=0)
        s_new = lax.dot_general(qbd, kn, _NT, preferred_element_type=jnp.float32)
        dn = _exact_dot(upper, lfn_ref[0], x_on_left=True) + dcarry_ref[...]
        s_new = s_new - expand_heads(dn)
        q_t = lax.broadcasted_iota(jnp.int32, (ht, PAGE_SIZE), 0) % dec_seq
        k_t = lax.broadcasted_iota(jnp.int32, (ht, PAGE_SIZE), 1)
        s_new = jnp.where(k_t <= q_t, s_new, NEG_INF)
        online_update(s_new, lambda p: jnp.dot(p, vn, preferred_element_type=jnp.float32))
        o = jnp.where(head_mask, acc_ref[...] / l_ref[...], 0.0)
        out = o[0:dec_seq, :]
        for h in range(1, ATTN_HEADS):
            out = out + o[h * dec_seq:(h + 1) * dec_seq, :]
        o_ref[...] = out.astype(o_ref.dtype)


def _sample_attn(page_table_flat, q, k_new, v_new, lf_new_t, cache_kt, cache_vt, cache_lft, *, n_pages, dec_seq):
    rows = q.shape[0]
    n_seq = rows // dec_seq
    npg = PAGES_PER_STEP
    assert n_pages % npg == 0
    n_chunks = n_pages // npg
    ht = ATTN_HEADS * dec_seq

    def page_spec(shape, j):
        return pl.BlockSpec((1,) + shape, lambda b, c, pt: (pt[b * n_pages + c * npg + j], 0, 0))

    seqblk = lambda w: pl.BlockSpec((dec_seq, w), lambda b, c, pt: (b, 0))
    in_specs = [seqblk(ATTN_WIDTH), seqblk(ATTN_WIDTH), seqblk(ATTN_WIDTH),
                pl.BlockSpec((1, ATTN_HEADS, PAGE_SIZE), lambda b, c, pt: (b, 0, 0))]
    in_specs += [page_spec((ATTN_WIDTH, PAGE_SIZE), j) for j in range(npg)]
    in_specs += [page_spec((ATTN_WIDTH, PAGE_SIZE), j) for j in range(npg)]
    in_specs += [page_spec((ATTN_HEADS, PAGE_SIZE), j) for j in range(npg)]
    grid_spec = pltpu.PrefetchScalarGridSpec(
        num_scalar_prefetch=1, grid=(n_seq, n_chunks), in_specs=in_specs,
        out_specs=pl.BlockSpec((dec_seq, ATTN_WIDTH), lambda b, c, pt: (b, 0)),
        scratch_shapes=[pltpu.VMEM((ht, 1), jnp.float32), pltpu.VMEM((ht, 1), jnp.float32),
                        pltpu.VMEM((ht, ATTN_WIDTH), jnp.float32),
                        pltpu.VMEM((ATTN_HEADS, PAGE_SIZE), jnp.float32)])
    kernel = functools.partial(_sample_attn_kernel, n_chunks=n_chunks, dec_seq=dec_seq)
    return pl.pallas_call(
        kernel, grid_spec=grid_spec,
        out_shape=jax.ShapeDtypeStruct((rows, ATTN_WIDTH), jnp.bfloat16),
        compiler_params=pltpu.CompilerParams(dimension_semantics=("arbitrary", "arbitrary"),
                                             vmem_limit_bytes=VMEM_LIMIT_BYTES),
        name="sample_attn",
    )(page_table_flat, q, k_new, v_new, lf_new_t, *([cache_kt] * npg), *([cache_vt] * npg), *([cache_lft] * npg))


def _mix_ffn_kernel(x_ref, attn_ref, pool_ref, wo_ref, gf_ref, wg_ref, wu_ref, wd_ref, gl_ref,
                    y_ref, acc_ref, *, attn_transposed):
    bf16 = jnp.bfloat16
    if attn_transposed:
        mo = lax.dot_general(attn_ref[...], wo_ref[0:ATTN_WIDTH, :], (((0,), (0,)), ((), ())),
                             preferred_element_type=jnp.float32)
    else:
        mo = jnp.dot(attn_ref[...], wo_ref[0:ATTN_WIDTH, :], preferred_element_type=jnp.float32)
    mo = mo + jnp.dot(pool_ref[...], wo_ref[ATTN_WIDTH:, :], preferred_element_type=jnp.float32)
    x1 = x_ref[...] + mo
    h = _rms_norm(x1, gf_ref[...]).astype(bf16)
    acc_ref[...] = x1

    def chunk(c, carry):
        gate = jnp.dot(h, wg_ref[c], preferred_element_type=jnp.float32)
        up = jnp.dot(h, wu_ref[c], preferred_element_type=jnp.float32)
        act = (jax.nn.silu(gate) * up).astype(bf16)
        acc_ref[...] += jnp.dot(act, wd_ref[c], preferred_element_type=jnp.float32)
        return carry

    lax.fori_loop(0, wg_ref.shape[0], chunk, 0)
    y_ref[...] = _rms_norm(acc_ref[...], gl_ref[...])


def _mix_ffn(x2, attn, pool, wo, g_ffn, wg3, wu3, wd3, g_final, *, attn_transposed):
    rows, d = x2.shape
    rb = min(FFN_ROWS, rows)
    assert rows % rb == 0
    full = lambda *shape: _const_spec(shape)
    attn_spec = (pl.BlockSpec((ATTN_WIDTH, rb), lambda i: (0, i)) if attn_transposed
                 else pl.BlockSpec((rb, ATTN_WIDTH), lambda i: (i, 0)))
    kernel = functools.partial(_mix_ffn_kernel, attn_transposed=attn_transposed)
    return pl.pallas_call(
        kernel, grid=(rows // rb,),
        in_specs=[pl.BlockSpec((rb, d), lambda i: (i, 0)), attn_spec,
                  pl.BlockSpec((rb, POOL_WIDTH), lambda i: (i, 0)),
                  full(*wo.shape), full(*g_ffn.shape), full(*wg3.shape), full(*wu3.shape), full(*wd3.shape),
                  full(*g_final.shape)],
        out_specs=pl.BlockSpec((rb, d), lambda i: (i, 0)),
        out_shape=jax.ShapeDtypeStruct((rows, d), jnp.float32),
        scratch_shapes=[pltpu.VMEM((rb, d), jnp.float32)],
        compiler_params=pltpu.CompilerParams(dimension_semantics=("arbitrary",),
                                             vmem_limit_bytes=VMEM_LIMIT_BYTES),
        name="mix_ffn_t" if attn_transposed else "mix_ffn",
    )(x2, attn, pool, wo, g_ffn, wg3, wu3, wd3, g_final)


def kernel(x_prompt, x_sample, cache_k, cache_v, cache_logf, state_pool, page_table, meta_tokens, g_mix, w_in,
           b_f, w_pool, pool_scale, w_out, g_ffn, w_gate, w_up, w_down, g_final):
    f32, bf16 = jnp.float32, jnp.bfloat16
    batch, seq, d_model = x_prompt.shape
    dec_batch, dec_seq, _ = x_sample.shape
    depth, n_phys = cache_k.shape[0], cache_k.shape[1]
    n_pages = page_table.shape[1]
    d_ff = w_gate.shape[-1]
    assert batch == 1 and depth == 1 and d_ff % FFN_CHUNK == 0
    a = ATTN_WIDTH
    n_ch = d_ff // FFN_CHUNK

    w = w_in[0]
    w_q, w_k, w_v = w[:, 0:a], w[:, a:2 * a], w[:, 2 * a:3 * a]
    w_f, w_u = w[:, 3 * a:3 * a + ATTN_HEADS], w[:, 3 * a + ATTN_HEADS:]
    wnt = jnp.concatenate([w_q.T, w_k.T, w_v.T, jnp.pad(w_f.T, ((0, 16 - ATTN_HEADS), (0, 0)))], axis=0).astype(bf16)
    wk_aug = jnp.pad(w_k.reshape(d_model, ATTN_HEADS, HEAD_DIM), ((0, 0), (0, 0), (0, AUG_DEPTH - HEAD_DIM)))
    wk_aug = wk_aug.reshape(d_model, ATTN_HEADS * AUG_DEPTH).astype(bf16)
    wu = w_u.astype(bf16)
    wf_pad = jnp.pad(w_f, ((0, 0), (0, 128 - ATTN_HEADS))).astype(bf16)
    wqkv = w[:, 0:3 * a].astype(bf16)
    bf_row = jnp.pad(b_f.astype(f32), ((0, 0), (0, 128 - ATTN_HEADS)))
    bf_col = jnp.pad(b_f.astype(f32).T, ((0, 16 - ATTN_HEADS), (0, 0)))
    lane = jnp.arange(128)[:, None]
    col = jnp.arange(ATTN_HEADS * AUG_DEPTH)[None, :]
    sel = jnp.stack([(col == lane * AUG_DEPTH + HEAD_DIM + p) & (lane < ATTN_HEADS) for p in range(3)])
    sel = sel.astype(bf16)
    wpool = w_pool[0].astype(bf16)
    pscale = pool_scale.astype(f32)
    wo = w_out[0].astype(bf16)
    wg3 = w_gate[0].reshape(d_model, n_ch, FFN_CHUNK).transpose(1, 0, 2).astype(bf16)
    wu3 = w_up[0].reshape(d_model, n_ch, FFN_CHUNK).transpose(1, 0, 2).astype(bf16)
    wd3 = w_down[0].reshape(n_ch, FFN_CHUNK, d_model).astype(bf16)
    g_mix2, g_ffn2, g_final2 = g_mix.astype(f32), g_ffn.astype(f32), g_final.astype(f32)[None, :]

    xp2 = x_prompt[0]
    kt, vt, lft, ka, vta, qta, pool_p, u_last = _prompt_proj(
        xp2, meta_tokens.astype(f32), g_mix2, wnt, wk_aug, wu, wf_pad, bf_col, bf_row, sel, wpool, pscale)
    attn_t = _prompt_attn(qta, ka, vta)
    y_prompt = _mix_ffn(xp2, attn_t, pool_p, wo, g_ffn2, wg3, wu3, wd3, g_final2, attn_transposed=True)[None]
    l_tot = N_META + seq
    k_prompt = kt.reshape(ATTN_HEADS, HEAD_DIM, l_tot).transpose(2, 0, 1)[None, None]
    v_prompt = vt.reshape(ATTN_HEADS, HEAD_DIM, l_tot).transpose(2, 0, 1)[None, None]
    logf_prompt = lft.T[None, None]
    pool_prompt = u_last[HIST_ROWS - POOL_HIST:][None, None]

    rows = dec_batch * dec_seq
    xs2 = x_sample.reshape(rows, d_model)
    hist = jnp.pad(state_pool[0].astype(f32), ((0, 0), (HIST_ROWS - POOL_HIST, 0), (0, 0)))
    q_s, k_s, v_s, lf_s, u_s, pool_s = _sample_proj(
        xs2, hist, g_mix2, wqkv, wu, wf_pad, bf_row, wpool, pscale,
        past_len=n_pages * PAGE_SIZE, dec_seq=dec_seq)
    cache_kt = cache_k[0].transpose(0, 2, 3, 1).reshape(n_phys, a, PAGE_SIZE)
    cache_vt = cache_v[0].transpose(0, 2, 3, 1).reshape(n_phys, a, PAGE_SIZE)
    cache_lft = cache_logf[0].transpose(0, 2, 1)
    lf_new_t = jnp.pad(lf_s.reshape(dec_batch, dec_seq, ATTN_HEADS).transpose(0, 2, 1),
                       ((0, 0), (0, 0), (0, PAGE_SIZE - dec_seq)))
    attn_s = _sample_attn(page_table.reshape(-1).astype(jnp.int32), q_s, k_s, v_s, lf_new_t,
                          cache_kt, cache_vt, cache_lft, n_pages=n_pages, dec_seq=dec_seq)
    y_sample = _mix_ffn(xs2, attn_s, pool_s, wo, g_ffn2, wg3, wu3, wd3, g_final2, attn_transposed=False)
    y_sample = y_sample.reshape(dec_batch, dec_seq, d_model)
    shp = (1, dec_batch, dec_seq, ATTN_HEADS, HEAD_DIM)
    k_sample, v_sample = k_s.reshape(shp), v_s.reshape(shp)
    logf_sample = lf_s.reshape(1, dec_batch, dec_seq, ATTN_HEADS)
    u_ext = jnp.concatenate([state_pool[0].astype(f32), u_s.reshape(dec_batch, dec_seq, POOL_WIDTH)], axis=1)
    pool_sample = u_ext[:, -POOL_HIST:][None]
    return (y_prompt, y_sample, k_prompt, v_prompt, logf_prompt, pool_prompt,
            k_sample, v_sample, logf_sample, pool_sample)
```
